```python
import jax
import jax.numpy as jnp
from jax import lax
import numpy as np

D_MODEL = 1024
BATCH = 4
SEQ = 8192
DEPTH = 2

MLA_HEADS = 8
MLA_Q_LORA = 384
MLA_KV_LORA = 256
MLA_NOPE_DIM = 64
MLA_ROPE_DIM = 32
MLA_V_DIM = 64
MLA_WIDTH = MLA_HEADS * MLA_V_DIM
ATTN_Q_BLOCK = 128

LRU_WIDTH = 512
LRU_BLOCKS = 8
LRU_BLOCK_DIM = LRU_WIDTH // LRU_BLOCKS
LRU_CONV_WIDTH = 4
LRU_C = 8.0

RET_HEADS = 8
RET_KEY_DIM = 64
RET_VAL_DIM = 64
RET_WIDTH = RET_HEADS * RET_VAL_DIM
RET_CHUNK = 128

N_BRANCHES = 3
BRANCH_WIDTH = 512
ROPE_BASE = 10000.0
LN_EPS = 1e-5
RMS_EPS = 1e-6
DEEPNORM_ALPHA = (2 * DEPTH) ** 0.25
DEEPNORM_BETA = (8 * DEPTH) ** -0.25

IN_SIZES = (MLA_Q_LORA, MLA_KV_LORA, MLA_ROPE_DIM, MLA_WIDTH,
            LRU_WIDTH, LRU_WIDTH,
            RET_HEADS * RET_KEY_DIM, RET_HEADS * RET_KEY_DIM, RET_WIDTH, RET_WIDTH,
            N_BRANCHES * D_MODEL)
IN_COLS = sum(IN_SIZES)
IN_SPLITS = tuple(int(s) for s in np.cumsum(IN_SIZES)[:-1])

kernel_name = 'hybrid_mla_rglru_retention_deepnorm'


def _layer_norm(x, g, b):
    xf = x.astype(jnp.float32)
    mu = jnp.mean(xf, -1, keepdims=True)
    var = jnp.mean(jnp.square(xf - mu), -1, keepdims=True)
    return ((xf - mu) * lax.rsqrt(var + LN_EPS) * g + b).astype(x.dtype)


def _rms_norm(x, g):
    xf = x.astype(jnp.float32)
    return (xf * lax.rsqrt(jnp.mean(jnp.square(xf), -1, keepdims=True) + RMS_EPS) * g).astype(x.dtype)


def _rope(x, pos):
    half = x.shape[-1] // 2
    inv_freq = ROPE_BASE ** (-jnp.arange(half, dtype=jnp.float32) / half)
    ang = pos.astype(jnp.float32)[:, :, None, None] * inv_freq
    cos, sin = jnp.cos(ang), jnp.sin(ang)
    x1 = x[..., :half].astype(jnp.float32)
    x2 = x[..., half:].astype(jnp.float32)
    return jnp.concatenate([x1 * cos - x2 * sin, x2 * cos + x1 * sin], -1).astype(x.dtype)


def _causal_block_attention(q, k, v, scale):
    B, S, H, dq = q.shape
    nb = S // ATTN_Q_BLOCK
    q_blocks = q.reshape(B, nb, ATTN_Q_BLOCK, H, dq).transpose(1, 0, 2, 3, 4)
    starts = jnp.arange(nb, dtype=jnp.int32) * ATTN_Q_BLOCK
    key_idx = jnp.arange(S, dtype=jnp.int32)

    def one_block(args):
        qb, s0 = args
        s = jnp.einsum('bqhd,bkhd->bhqk', qb, k).astype(jnp.float32) * scale
        q_idx = s0 + jnp.arange(ATTN_Q_BLOCK, dtype=jnp.int32)
        mask = key_idx[None, :] <= q_idx[:, None]
        p = jax.nn.softmax(jnp.where(mask, s, -jnp.inf), axis=-1).astype(v.dtype)
        return jnp.einsum('bhqk,bkhd->bqhd', p, v)

    out = lax.map(one_block, (q_blocks, starts))
    return out.transpose(1, 0, 2, 3, 4).reshape(B, S, H, v.shape[-1])


def _mla(q_lat, kv_lat, k_pe, pos, q_norm, w_uq, kv_norm, w_ukv):
    B, S, _ = q_lat.shape
    q = jnp.einsum('bsr,re->bse', _rms_norm(q_lat, q_norm), w_uq)
    q = q.reshape(B, S, MLA_HEADS, MLA_NOPE_DIM + MLA_ROPE_DIM)
    q = jnp.concatenate([q[..., :MLA_NOPE_DIM], _rope(q[..., MLA_NOPE_DIM:], pos)], -1)
    kv = jnp.einsum('bsr,re->bse', _rms_norm(kv_lat, kv_norm), w_ukv)
    kv = kv.reshape(B, S, MLA_HEADS, MLA_NOPE_DIM + MLA_V_DIM)
    k_nope, v = kv[..., :MLA_NOPE_DIM], kv[..., MLA_NOPE_DIM:]
    k_rot = jnp.broadcast_to(_rope(k_pe[:, :, None, :], pos), (B, S, MLA_HEADS, MLA_ROPE_DIM))
    k = jnp.concatenate([k_nope, k_rot], -1)
    o = _causal_block_attention(q, k, v, (MLA_NOPE_DIM + MLA_ROPE_DIM) ** -0.5)
    return o.reshape(B, S, MLA_WIDTH)


def _rglru(u, conv_w, conv_b, w_r, b_r, w_i, b_i, lam):
    B, S, W = u.shape
    xc = lax.conv_general_dilated(u, conv_w[:, None, :], window_strides=(1,),
                                  padding=[(LRU_CONV_WIDTH - 1, 0)],
                                  dimension_numbers=('NWC', 'WIO', 'NWC'),
                                  feature_group_count=W) + conv_b
    xh = xc.reshape(B, S, LRU_BLOCKS, LRU_BLOCK_DIM)
    r = jax.nn.sigmoid(jnp.einsum('bshi,hij->bshj', xh, w_r) + b_r.reshape(LRU_BLOCKS, LRU_BLOCK_DIM)).reshape(B, S, W)
    gi = jax.nn.sigmoid(jnp.einsum('bshi,hij->bshj', xh, w_i) + b_i.reshape(LRU_BLOCKS, LRU_BLOCK_DIM)).reshape(B, S, W)
    log_a = -LRU_C * r.astype(jnp.float32) * jax.nn.softplus(-lam.astype(jnp.float32))
    a = jnp.exp(log_a)
    b = jnp.sqrt(-jnp.expm1(2.0 * log_a)) * (gi * xc).astype(jnp.float32)

    def combine(left, right):
        a1, b1 = left
        a2, b2 = right
        return a1 * a2, a2 * b1 + b2

    _, h = lax.associative_scan(combine, (a, b), axis=1)
    return h.astype(u.dtype)


def _retention(q, k, v, pos, gn_g):
    B, S, _ = q.shape
    H, C = RET_HEADS, RET_CHUNK
    N = S // C
    q = _rope(q.reshape(B, S, H, RET_KEY_DIM), pos)
    k = _rope(k.reshape(B, S, H, RET_KEY_DIM), pos) * (RET_KEY_DIM ** -0.5)
    v = v.reshape(B, S, H, RET_VAL_DIM)
    log_gamma = jnp.log1p(-jnp.exp2(-5.0 - jnp.arange(H, dtype=jnp.float32)))
    idx = jnp.arange(C, dtype=jnp.float32)
    diff = idx[:, None] - idx[None, :]
    decay = jnp.where(diff[None] >= 0, jnp.exp(diff[None] * log_gamma[:, None, None]), 0.0)
    qc = q.reshape(B, N, C, H, RET_KEY_DIM)
    kc = k.reshape(B, N, C, H, RET_KEY_DIM)
    vc = v.reshape(B, N, C, H, RET_VAL_DIM)
    scores = jnp.einsum('bnihd,bnjhd->bnhij', qc, kc) * decay.astype(q.dtype)
    inner = jnp.einsum('bnhij,bnjhe->bnihe', scores, vc)
    k_w = jnp.exp((C - 1.0 - idx)[:, None] * log_gamma[None, :])
    chunk_kv = jnp.einsum('bnjhd,bnjhe->bnhde', kc * k_w[:, :, None].astype(k.dtype), vc).astype(jnp.float32)
    chunk_decay = jnp.exp(C * log_gamma)[None, :, None, None]

    def step(state, kv_n):
        return chunk_decay * state + kv_n, state

    _, prev = lax.scan(step, jnp.zeros((B, H, RET_KEY_DIM, RET_VAL_DIM), jnp.float32),
                       jnp.moveaxis(chunk_kv, 1, 0))
    prev = jnp.moveaxis(prev, 0, 1).astype(q.dtype)
    q_w = jnp.exp((idx + 1.0)[:, None] * log_gamma[None, :])
    cross = jnp.einsum('bnihd,bnhde->bnihe', qc * q_w[:, :, None].astype(q.dtype), prev)
    o = (inner + cross).reshape(B, S, H, RET_VAL_DIM).astype(jnp.float32)
    mu = jnp.mean(o, -1, keepdims=True)
    var = jnp.mean(jnp.square(o - mu), -1, keepdims=True)
    o = (o - mu) * lax.rsqrt(var + LN_EPS) * gn_g.reshape(H, RET_VAL_DIM)
    return o.reshape(B, S, RET_WIDTH).astype(v.dtype)


def setup_inputs(seed: int = 0) -> dict:
    key = jax.random.key(seed)
    ks = jax.random.split(key, 20)
    f32 = jnp.float32

    def nrm(k, shape, scale):
        return jax.random.normal(k, shape, f32) * scale

    x = nrm(ks[0], (BATCH, SEQ, D_MODEL), 1.0)
    positions = (jnp.arange(SEQ, dtype=jnp.int32)[None, :]
                 + jax.random.randint(ks[1], (BATCH, 1), 0, 1024, dtype=jnp.int32))
    w_in = nrm(ks[2], (DEPTH, D_MODEL, IN_COLS), D_MODEL ** -0.5)
    b_merge = nrm(ks[3], (DEPTH, N_BRANCHES * D_MODEL), 0.02)
    mla_q_norm = 1.0 + nrm(ks[4], (DEPTH, MLA_Q_LORA), 0.02)
    mla_w_uq = nrm(ks[5], (DEPTH, MLA_Q_LORA, MLA_HEADS * (MLA_NOPE_DIM + MLA_ROPE_DIM)), MLA_Q_LORA ** -0.5)
    mla_kv_norm = 1.0 + nrm(ks[6], (DEPTH, MLA_KV_LORA), 0.02)
    mla_w_ukv = nrm(ks[7], (DEPTH, MLA_KV_LORA, MLA_HEADS * (MLA_NOPE_DIM + MLA_V_DIM)), MLA_KV_LORA ** -0.5)
    lru_conv_w = nrm(ks[8], (DEPTH, LRU_CONV_WIDTH, LRU_WIDTH), LRU_CONV_WIDTH ** -0.5)
    lru_conv_b = nrm(ks[9], (DEPTH, LRU_WIDTH), 0.02)
    lru_w_r = nrm(ks[10], (DEPTH, LRU_BLOCKS, LRU_BLOCK_DIM, LRU_BLOCK_DIM), LRU_BLOCK_DIM ** -0.5)
    lru_b_r = nrm(ks[11], (DEPTH, LRU_WIDTH), 0.02)
    lru_w_i = nrm(ks[12], (DEPTH, LRU_BLOCKS, LRU_BLOCK_DIM, LRU_BLOCK_DIM), LRU_BLOCK_DIM ** -0.5)
    lru_b_i = nrm(ks[13], (DEPTH, LRU_WIDTH), 0.02)
    a_c = jax.random.uniform(ks[14], (DEPTH, LRU_WIDTH), f32, minval=0.9, maxval=0.999)
    a0 = a_c ** (1.0 / LRU_C)
    lru_lambda = jnp.log(a0) - jnp.log1p(-a0)
    ret_gn_g = 1.0 + nrm(ks[15], (DEPTH, RET_WIDTH), 0.02)
    w_branch = nrm(ks[16], (DEPTH, N_BRANCHES, BRANCH_WIDTH, D_MODEL), BRANCH_WIDTH ** -0.5 * DEEPNORM_BETA)
    w_out = nrm(ks[17], (DEPTH, D_MODEL, D_MODEL), D_MODEL ** -0.5 * DEEPNORM_BETA)
    ln_g = 1.0 + nrm(ks[18], (DEPTH, D_MODEL), 0.02)
    ln_b = nrm(ks[19], (DEPTH, D_MODEL), 0.02)
    return {'x': x, 'positions': positions, 'w_in': w_in, 'b_merge': b_merge,
            'mla_q_norm': mla_q_norm, 'mla_w_uq': mla_w_uq, 'mla_kv_norm': mla_kv_norm, 'mla_w_ukv': mla_w_ukv,
            'lru_conv_w': lru_conv_w, 'lru_conv_b': lru_conv_b, 'lru_w_r': lru_w_r, 'lru_b_r': lru_b_r,
            'lru_w_i': lru_w_i, 'lru_b_i': lru_b_i, 'lru_lambda': lru_lambda, 'ret_gn_g': ret_gn_g,
            'w_branch': w_branch, 'w_out': w_out, 'ln_g': ln_g, 'ln_b': ln_b}


def reference(x, positions, w_in, b_merge, mla_q_norm, mla_w_uq, mla_kv_norm, mla_w_ukv,
              lru_conv_w, lru_conv_b, lru_w_r, lru_b_r, lru_w_i, lru_b_i, lru_lambda, ret_gn_g,
              w_branch, w_out, ln_g, ln_b):
    B, S, _ = x.shape
    for l in range(DEPTH):
        z = jnp.einsum('bsd,de->bse', x, w_in[l])
        (q_lat, kv_lat, k_pe, g_mla, lru_u, g_lru,
         r_q, r_k, r_v, g_ret, merge_logits) = jnp.split(z, IN_SPLITS, axis=-1)
        y_mla = _mla(q_lat, kv_lat, k_pe, positions, mla_q_norm[l], mla_w_uq[l],
                     mla_kv_norm[l], mla_w_ukv[l]) * jax.nn.silu(g_mla)
        y_lru = _rglru(lru_u, lru_conv_w[l], lru_conv_b[l], lru_w_r[l], lru_b_r[l],
                       lru_w_i[l], lru_b_i[l], lru_lambda[l]) * jax.nn.silu(g_lru)
        y_ret = _retention(r_q, r_k, r_v, positions, ret_gn_g[l]) * jax.nn.silu(g_ret)
        gates = jax.nn.sigmoid(merge_logits + b_merge[l]).reshape(B, S, N_BRANCHES, D_MODEL)
        mixed = (gates[:, :, 0] * jnp.einsum('bse,ed->bsd', y_mla, w_branch[l, 0])
                 + gates[:, :, 1] * jnp.einsum('bse,ed->bsd', y_lru, w_branch[l, 1])
                 + gates[:, :, 2] * jnp.einsum('bse,ed->bsd', y_ret, w_branch[l, 2]))
        out = jnp.einsum('bsd,de->bse', mixed, w_out[l])
        x = _layer_norm(DEEPNORM_ALPHA * x + out, ln_g[l], ln_b[l])
    return x
```

```python
import functools

import jax
import jax.numpy as jnp
from jax import lax
from jax.experimental import pallas as pl
from jax.experimental.pallas import tpu as pltpu

F32 = jnp.float32
BF16 = jnp.bfloat16

D_MODEL = 1024
DEPTH = 2
HEADS = 8
MLA_Q_LORA = 384
MLA_KV_LORA = 256
MLA_NOPE = 64
MLA_ROPE = 32
MLA_V = 64
HEAD_PAD = 128
WIDTH = 512
LRU_BLOCKS = 8
LRU_CONV = 4
LRU_C = 8.0
RET_DK = 64
RET_CHUNK = 128
N_BRANCHES = 3
ROPE_BASE = 10000.0
LN_EPS = 1e-5
RMS_EPS = 1e-6
ALPHA = (2 * DEPTH) ** 0.25
ATTN_SCALE = (MLA_NOPE + MLA_ROPE) ** -0.5
NEG = -1e30

IN_SIZES = (MLA_Q_LORA, MLA_KV_LORA, MLA_ROPE, WIDTH, WIDTH, WIDTH, WIDTH, WIDTH, WIDTH, WIDTH,
            N_BRANCHES * D_MODEL)

VMEM_LIMIT = 56 * 1024 * 1024


def _const_spec(shape):
    return pl.BlockSpec(shape, lambda *_: (0,) * len(shape))


def _params(sem, vmem=VMEM_LIMIT):
    return pltpu.CompilerParams(dimension_semantics=sem, vmem_limit_bytes=vmem)


def _dot(a, b):
    return jnp.dot(a, b, preferred_element_type=F32)


def _dot_nt(a, b):
    return lax.dot_general(a, b, (((1,), (1,)), ((), ())), preferred_element_type=F32)


def _dot_tn(a, b):
    return lax.dot_general(a, b, (((0,), (0,)), ((), ())), preferred_element_type=F32)


def _sigmoid(x):
    return 1.0 / (1.0 + jnp.exp(-x))


def _silu(x):
    return x * _sigmoid(x)


def _rope_table_kernel(pos_ref, freq_ref, sgn_ref, cos_ref, sin_ref):
    ang = pos_ref[0].astype(F32) * freq_ref[...]
    act = sgn_ref[...] != 0.0
    cos_ref[0] = jnp.where(act, jnp.cos(ang), 1.0)
    sin_ref[0] = jnp.sin(ang) * sgn_ref[...]


def _rope_tables(positions, freq, sgn, ts=1024):
    B, S = positions.shape
    pos3 = positions.reshape(B, S, 1)
    tab = jax.ShapeDtypeStruct((B, S, 128), F32)
    return pl.pallas_call(
        _rope_table_kernel,
        out_shape=(tab, tab),
        grid=(B, S // ts),
        in_specs=[pl.BlockSpec((1, ts, 1), lambda b, s: (b, s, 0)),
                  _const_spec((1, 128)), _const_spec((1, 128))],
        out_specs=(pl.BlockSpec((1, ts, 128), lambda b, s: (b, s, 0)),
                   pl.BlockSpec((1, ts, 128), lambda b, s: (b, s, 0))),
        compiler_params=_params(("parallel", "parallel")),
        name="rope_tables",
    )(pos3, freq, sgn)


def _rope_lanes(x, cos, sin, lo_mask, half):
    swapped = jnp.where(lo_mask, pltpu.roll(x, 128 - half, 1), pltpu.roll(x, half, 1))
    return x * cos + swapped * sin


def _mla_proj_kernel(x_ref, cos_ref, sin_ref, wlat_ref, qn_ref, kvn_ref, wuq_ref, wuk_ref, wuv_ref,
                     q_ref, k_ref, v_ref):
    x = x_ref[0].astype(BF16)
    lat = _dot(x, wlat_ref[...])
    q_lat = lat[:, :MLA_Q_LORA]
    kv_lat = lat[:, MLA_Q_LORA:MLA_Q_LORA + MLA_KV_LORA]
    k_pe = lat[:, MLA_Q_LORA + MLA_KV_LORA:]

    def rms(v, g):
        return v * lax.rsqrt(jnp.mean(jnp.square(v), -1, keepdims=True) + RMS_EPS) * g

    cos = cos_ref[0]
    sin = sin_ref[0]
    lane = lax.broadcasted_iota(jnp.int32, cos.shape, 1)
    lo = lane < MLA_NOPE + MLA_ROPE // 2

    q = _dot(rms(q_lat, qn_ref[...]).astype(BF16), wuq_ref[...])
    kvn = rms(kv_lat, kvn_ref[...]).astype(BF16)
    k_nope = _dot(kvn, wuk_ref[...])
    v_ref[0] = _dot(kvn, wuv_ref[...]).astype(BF16)
    k_rot = _rope_lanes(k_pe, cos, sin, lo, MLA_ROPE // 2)
    for h in range(HEADS):
        sl = slice(h * HEAD_PAD, (h + 1) * HEAD_PAD)
        qh = _rope_lanes(q[:, sl], cos, sin, lo, MLA_ROPE // 2)
        q_ref[0, :, sl] = (qh * ATTN_SCALE).astype(BF16)
        k_ref[0, :, sl] = (k_nope[:, sl] + k_rot).astype(BF16)


def _mla_proj(x, cos, sin, wlat, qn, kvn, wuq, wuk, wuv, tm=512):
    B, S, _ = x.shape
    tok = lambda c: pl.BlockSpec((1, tm, c), lambda b, s: (b, s, 0))
    return pl.pallas_call(
        _mla_proj_kernel,
        out_shape=(jax.ShapeDtypeStruct((B, S, HEADS * HEAD_PAD), BF16),
                   jax.ShapeDtypeStruct((B, S, HEADS * HEAD_PAD), BF16),
                   jax.ShapeDtypeStruct((B, S, WIDTH), BF16)),
        grid=(B, S // tm),
        in_specs=[tok(D_MODEL), tok(128), tok(128),
                  _const_spec(wlat.shape), _const_spec(qn.shape), _const_spec(kvn.shape),
                  _const_spec(wuq.shape), _const_spec(wuk.shape), _const_spec(wuv.shape)],
        out_specs=(tok(HEADS * HEAD_PAD), tok(HEADS * HEAD_PAD), tok(WIDTH)),
        compiler_params=_params(("parallel", "parallel")),
        name="mla_proj",
    )(x, cos, sin, wlat, qn, kvn, wuq, wuk, wuv)


def _attn_kernel(q_ref, k_ref, v_ref, o_ref, *, tq, tk):
    i = pl.program_id(2)
    n_full = i * (tq // tk)
    row = lax.broadcasted_iota(jnp.int32, (tq, tk), 0)
    col = lax.broadcasted_iota(jnp.int32, (tq, tk), 1)
    for hh in range(2):
        q = q_ref[0, :, hh * HEAD_PAD:(hh + 1) * HEAD_PAD]

        def step(j, carry, masked):
            m, l, acc = carry
            start = pl.multiple_of(j * tk, tk)
            k = k_ref[0, pl.ds(start, tk), hh * HEAD_PAD:(hh + 1) * HEAD_PAD]
            v = v_ref[0, pl.ds(start, tk), hh * MLA_V:(hh + 1) * MLA_V]
            s = _dot_nt(q, k)
            if masked is not None:
                s = jnp.where(col + masked * tk <= row, s, NEG)
            m_new = jnp.maximum(m, jnp.max(s, -1, keepdims=True))
            p = jnp.exp(s - m_new)
            a = jnp.exp(m - m_new)
            l = a * l + jnp.sum(p, -1, keepdims=True)
            acc = a * acc + _dot(p.astype(BF16), v)
            return m_new, l, acc

        carry = (jnp.full((tq, 1), NEG, F32), jnp.zeros((tq, 1), F32), jnp.zeros((tq, MLA_V), F32))
        carry = lax.fori_loop(0, n_full, lambda j, c: step(j, c, None), carry)
        for d in range(tq // tk):
            carry = step(n_full + d, carry, d)
        _, l, acc = carry
        o_ref[0, :, hh * MLA_V:(hh + 1) * MLA_V] = (acc / l).astype(o_ref.dtype)


def _attention(q, k, v, tq=512, tk=256):
    B, S, _ = q.shape
    return pl.pallas_call(
        functools.partial(_attn_kernel, tq=tq, tk=tk),
        out_shape=jax.ShapeDtypeStruct((B, S, WIDTH), BF16),
        grid=(B, HEADS // 2, S // tq),
        in_specs=[pl.BlockSpec((1, tq, 2 * HEAD_PAD), lambda b, h, i: (b, i, h)),
                  pl.BlockSpec((1, S, 2 * HEAD_PAD), lambda b, h, i: (b, 0, h)),
                  pl.BlockSpec((1, S, 2 * MLA_V), lambda b, h, i: (b, 0, h))],
        out_specs=pl.BlockSpec((1, tq, 2 * MLA_V), lambda b, h, i: (b, i, h)),
        compiler_params=_params(("parallel", "parallel", "arbitrary")),
        name="mla_attention",
    )(q, k, v)


def _shift_rows(x, d, fill):
    row = lax.broadcasted_iota(jnp.int32, x.shape, 0)
    return jnp.where(row < d, fill, pltpu.roll(x, d, 0))


def _lru_kernel(x_ref, w_ref, cw_ref, cb_ref, wg_ref, bg_ref, lam_ref, y_ref, ubuf, hcar, *, ts):
    @pl.when(pl.program_id(1) == 0)
    def _():
        ubuf[0:8, :] = jnp.zeros((8, WIDTH), F32)
        hcar[...] = jnp.zeros_like(hcar)

    z = _dot(x_ref[0].astype(BF16), w_ref[...])
    u = z[:, :WIDTH]
    g = z[:, WIDTH:]
    ubuf[8:8 + ts, :] = u
    xc = cb_ref[...] + cw_ref[3:4, :] * u
    for j in range(1, LRU_CONV):
        xc = xc + cw_ref[3 - j:4 - j, :] * ubuf[8 - j:8 - j + ts, :]
    ubuf[0:8, :] = u[ts - 8:, :]

    gz = _dot(xc.astype(BF16), wg_ref[...]) + bg_ref[...]
    r = _sigmoid(gz[:, :WIDTH])
    gi = _sigmoid(gz[:, WIDTH:])
    lam = lam_ref[...]
    softplus_neg = jnp.maximum(-lam, 0.0) + jnp.log1p(jnp.exp(-jnp.abs(lam)))
    log_a = -LRU_C * r * softplus_neg
    a = jnp.exp(log_a)
    t = jnp.tanh(log_a)
    b = jnp.sqrt(-2.0 * t / (1.0 - t)) * (gi * xc)

    d = 1
    while d < ts:
        b = a * _shift_rows(b, d, 0.0) + b
        a = a * _shift_rows(a, d, 1.0)
        d *= 2
    h = a * hcar[0:1, :] + b
    hcar[...] = jnp.broadcast_to(h[ts - 1:ts, :], hcar.shape)
    y_ref[0] = (h * _silu(g)).astype(y_ref.dtype)


def _lru(x, w, cw, cb, wg, bg, lam, ts=256):
    B, S, _ = x.shape
    return pl.pallas_call(
        functools.partial(_lru_kernel, ts=ts),
        out_shape=jax.ShapeDtypeStruct((B, S, WIDTH), BF16),
        grid=(B, S // ts),
        in_specs=[pl.BlockSpec((1, ts, D_MODEL), lambda b, s: (b, s, 0)),
                  _const_spec(w.shape), _const_spec(cw.shape), _const_spec(cb.shape),
                  _const_spec(wg.shape), _const_spec(bg.shape), _const_spec(lam.shape)],
        out_specs=pl.BlockSpec((1, ts, WIDTH), lambda b, s: (b, s, 0)),
        scratch_shapes=[pltpu.VMEM((ts + 8, WIDTH), F32), pltpu.VMEM((8, WIDTH), F32)],
        compiler_params=_params(("parallel", "arbitrary")),
        name="rglru",
    )(x, w, cw, cb, wg, bg, lam)


def _ret_kernel(x_ref, cos_ref, sin_ref, w_ref, decay_ref, qw_ref, kw_ref, cd_ref, gn_ref, y_ref,
                q_s, k_s, v_s, o_s, state, *, ts):
    C = RET_CHUNK

    @pl.when(pl.program_id(1) == 0)
    def _():
        state[...] = jnp.zeros_like(state)

    z = _dot(x_ref[0].astype(BF16), w_ref[...])
    cos = cos_ref[0]
    sin = sin_ref[0]
    lane = lax.broadcasted_iota(jnp.int32, cos.shape, 1)
    lo = (lane % RET_DK) < RET_DK // 2
    for t in range(WIDTH // 128):
        sl = slice(t * 128, (t + 1) * 128)
        q_s[:, sl] = _rope_lanes(z[:, sl], cos, sin, lo, RET_DK // 2)
        k_s[:, sl] = _rope_lanes(z[:, WIDTH + t * 128:WIDTH + (t + 1) * 128], cos, sin, lo,
                                 RET_DK // 2) * (RET_DK ** -0.5)
    v_s[...] = z[:, 2 * WIDTH:3 * WIDTH]
    gate = z[:, 3 * WIDTH:]

    def chunk(c, _):
        rows = pl.ds(pl.multiple_of(c * C, C), C)
        qc = q_s[rows, :]
        kc = k_s[rows, :]
        vc = v_s[rows, :].astype(BF16)
        qc_w = (qc * qw_ref[...]).astype(BF16)
        kc_w = (kc * kw_ref[...]).astype(BF16)
        qc = qc.astype(BF16)
        kc = kc.astype(BF16)
        for h in range(HEADS):
            hs = slice(h * RET_DK, (h + 1) * RET_DK)
            vh = vc[:, hs]
            scores = _dot_nt(qc[:, hs], kc[:, hs]) * decay_ref[h]
            prev = state[h]
            o = _dot(scores.astype(BF16), vh) + _dot(qc_w[:, hs], prev.astype(BF16))
            state[h] = cd_ref[:, hs] * prev + _dot_tn(kc_w[:, hs], vh)
            mu = jnp.mean(o, -1, keepdims=True)
            var = jnp.mean(jnp.square(o - mu), -1, keepdims=True)
            o_s[rows, hs] = (o - mu) * lax.rsqrt(var + LN_EPS) * gn_ref[:, hs]
        return 0

    lax.fori_loop(0, ts // C, chunk, 0)
    y_ref[0] = (o_s[...] * _silu(gate)).astype(y_ref.dtype)


def _retention(x, cos, sin, w, decay, qw, kw, cd, gn, ts=512):
    B, S, _ = x.shape
    tok = lambda c: pl.BlockSpec((1, ts, c), lambda b, s: (b, s, 0))
    return pl.pallas_call(
        functools.partial(_ret_kernel, ts=ts),
        out_shape=jax.ShapeDtypeStruct((B, S, WIDTH), BF16),
        grid=(B, S // ts),
        in_specs=[tok(D_MODEL), tok(128), tok(128), _const_spec(w.shape), _const_spec(decay.shape),
                  _const_spec(qw.shape), _const_spec(kw.shape), _const_spec(cd.shape),
                  _const_spec(gn.shape)],
        out_specs=tok(WIDTH),
        scratch_shapes=[pltpu.VMEM((ts, WIDTH), F32), pltpu.VMEM((ts, WIDTH), F32),
                        pltpu.VMEM((ts, WIDTH), F32), pltpu.VMEM((ts, WIDTH), F32),
                        pltpu.VMEM((HEADS, RET_DK, RET_DK), F32)],
        compiler_params=_params(("parallel", "arbitrary")),
        name="retention",
    )(x, cos, sin, w, decay, qw, kw, cd, gn)


def _merge_kernel(x_ref, o_ref, yl_ref, yr_ref, wg_ref, bm_ref, wb_ref, wo_ref, lg_ref, lb_ref, out_ref):
    x = x_ref[0]
    zg = _dot(x.astype(BF16), wg_ref[...])
    y_mla = (o_ref[0].astype(F32) * _silu(zg[:, :WIDTH])).astype(BF16)
    ys = (y_mla, yl_ref[0], yr_ref[0])
    mixed = None
    for n in range(N_BRANCHES):
        lo = WIDTH + n * D_MODEL
        gate = _sigmoid(zg[:, lo:lo + D_MODEL] + bm_ref[:, n * D_MODEL:(n + 1) * D_MODEL])
        term = gate * _dot(ys[n], wb_ref[n])
        mixed = term if mixed is None else mixed + term
    xf = ALPHA * x + _dot(mixed.astype(BF16), wo_ref[...])
    mu = jnp.mean(xf, -1, keepdims=True)
    var = jnp.mean(jnp.square(xf - mu), -1, keepdims=True)
    out_ref[0] = (xf - mu) * lax.rsqrt(var + LN_EPS) * lg_ref[...] + lb_ref[...]


def _merge(x, o, yl, yr, wg, bm, wb, wo, lg, lb, tm=256):
    B, S, _ = x.shape
    tok = lambda c: pl.BlockSpec((1, tm, c), lambda b, s: (b, s, 0))
    return pl.pallas_call(
        _merge_kernel,
        out_shape=jax.ShapeDtypeStruct((B, S, D_MODEL), F32),
        grid=(B, S // tm),
        in_specs=[tok(D_MODEL), tok(WIDTH), tok(WIDTH), tok(WIDTH),
                  _const_spec(wg.shape), _const_spec(bm.shape), _const_spec(wb.shape),
                  _const_spec(wo.shape), _const_spec(lg.shape), _const_spec(lb.shape)],
        out_specs=tok(D_MODEL),
        compiler_params=_params(("parallel", "parallel")),
        name="merge_out_ln",
    )(x, o, yl, yr, wg, bm, wb, wo, lg, lb)


def _rope_patterns():
    lane = jnp.arange(128)
    half = MLA_ROPE // 2
    inv = ROPE_BASE ** (-jnp.arange(half, dtype=F32) / half)
    in_rope = (lane >= MLA_NOPE) & (lane < MLA_NOPE + MLA_ROPE)
    idx = jnp.clip(lane - MLA_NOPE, 0, MLA_ROPE - 1) % half
    mla_freq = jnp.where(in_rope, inv[idx], 0.0).astype(F32)[None, :]
    mla_sgn = jnp.where(in_rope, jnp.where(lane < MLA_NOPE + half, -1.0, 1.0), 0.0).astype(F32)[None, :]
    half = RET_DK // 2
    inv = ROPE_BASE ** (-jnp.arange(half, dtype=F32) / half)
    ret_freq = inv[lane % half].astype(F32)[None, :]
    ret_sgn = jnp.where((lane % RET_DK) < half, -1.0, 1.0).astype(F32)[None, :]
    return mla_freq, mla_sgn, ret_freq, ret_sgn


def _retention_constants():
    C = RET_CHUNK
    log_gamma = jnp.log1p(-jnp.exp2(-5.0 - jnp.arange(HEADS, dtype=F32)))
    idx = jnp.arange(C, dtype=F32)
    diff = idx[:, None] - idx[None, :]
    decay = jnp.where(diff[None] >= 0, jnp.exp(diff[None] * log_gamma[:, None, None]), 0.0)
    k_w = jnp.exp((C - 1.0 - idx)[:, None] * log_gamma[None, :])
    q_w = jnp.exp((idx + 1.0)[:, None] * log_gamma[None, :])
    chunk_decay = jnp.exp(C * log_gamma)
    expand = lambda t: jnp.repeat(t, RET_DK, axis=-1)
    return decay, expand(q_w), expand(k_w), expand(chunk_decay[None, :])


def _block_diag(w):
    n, a, b = w.shape
    return (jnp.eye(n, dtype=w.dtype)[:, None, :, None] * w[:, :, None, :]).reshape(n * a, n * b)


def kernel(x, positions, w_in, b_merge, mla_q_norm, mla_w_uq, mla_kv_norm, mla_w_ukv, lru_conv_w, lru_conv_b,
           lru_w_r, lru_b_r, lru_w_i, lru_b_i, lru_lambda, ret_gn_g, w_branch, w_out, ln_g, ln_b):
    mla_freq, mla_sgn, ret_freq, ret_sgn = _rope_patterns()
    mla_cos, mla_sin = _rope_tables(positions, mla_freq, mla_sgn)
    ret_cos, ret_sin = _rope_tables(positions, ret_freq, ret_sgn)
    decay, q_w, k_w, chunk_decay = _retention_constants()

    offs = [0]
    for s in IN_SIZES:
        offs.append(offs[-1] + s)

    for l in range(DEPTH):
        w = w_in[l]
        col = lambda i, j=None: w[:, offs[i]:offs[(i if j is None else j) + 1]]
        zeros = lambda n: jnp.zeros((D_MODEL, n), F32)
        w_lat = jnp.concatenate([col(0), col(1), zeros(MLA_NOPE), col(2), zeros(HEAD_PAD - MLA_NOPE - MLA_ROPE)],
                                axis=1).astype(BF16)
        w_uq = jnp.pad(mla_w_uq[l].reshape(MLA_Q_LORA, HEADS, MLA_NOPE + MLA_ROPE),
                       ((0, 0), (0, 0), (0, HEAD_PAD - MLA_NOPE - MLA_ROPE)))
        w_uq = w_uq.reshape(MLA_Q_LORA, HEADS * HEAD_PAD).astype(BF16)
        w_ukv = mla_w_ukv[l].reshape(MLA_KV_LORA, HEADS, MLA_NOPE + MLA_V)
        w_uk = jnp.pad(w_ukv[:, :, :MLA_NOPE], ((0, 0), (0, 0), (0, HEAD_PAD - MLA_NOPE)))
        w_uk = w_uk.reshape(MLA_KV_LORA, HEADS * HEAD_PAD).astype(BF16)
        w_uv = w_ukv[:, :, MLA_NOPE:].reshape(MLA_KV_LORA, WIDTH).astype(BF16)

        q, k, v = _mla_proj(x, mla_cos, mla_sin, w_lat, mla_q_norm[l][None, :], mla_kv_norm[l][None, :],
                            w_uq, w_uk, w_uv)
        o_mla = _attention(q, k, v)

        w_gates = jnp.concatenate([_block_diag(lru_w_r[l]), _block_diag(lru_w_i[l])], axis=1).astype(BF16)
        b_gates = jnp.concatenate([lru_b_r[l], lru_b_i[l]])[None, :]
        y_lru = _lru(x, col(4, 5).astype(BF16), lru_conv_w[l], lru_conv_b[l][None, :], w_gates, b_gates,
                     lru_lambda[l][None, :])

        y_ret = _retention(x, ret_cos, ret_sin, col(6, 9).astype(BF16), decay, q_w, k_w, chunk_decay,
                           ret_gn_g[l][None, :])

        w_gm = jnp.concatenate([col(3), col(10)], axis=1).astype(BF16)
        x = _merge(x, o_mla, y_lru, y_ret, w_gm, b_merge[l][None, :], w_branch[l].astype(BF16),
                   w_out[l].astype(BF16), ln_g[l][None, :], ln_b[l][None, :])
    return x
```

```python
import functools

import jax
import jax.numpy as jnp
from jax import lax
from jax.experimental import pallas as pl
from jax.experimental.pallas import tpu as pltpu

F32 = jnp.float32
BF16 = jnp.bfloat16

D_MODEL = 1024
DEPTH = 2
HEADS = 8
MLA_Q_LORA = 384
MLA_KV_LORA = 256
MLA_NOPE = 64
MLA_ROPE = 32
MLA_V = 64
HEAD_PAD = 128
WIDTH = 512
LRU_BLOCKS = 8
LRU_CONV = 4
LRU_C = 8.0
RET_DK = 64
RET_CHUNK = 128
N_BRANCHES = 3
ROPE_BASE = 10000.0
LN_EPS = 1e-5
RMS_EPS = 1e-6
ALPHA = (2 * DEPTH) ** 0.25
ATTN_SCALE = (MLA_NOPE + MLA_ROPE) ** -0.5
NEG = -1e30
LOG2E = 1.4426950408889634

IN_SIZES = (MLA_Q_LORA, MLA_KV_LORA, MLA_ROPE, WIDTH, WIDTH, WIDTH, WIDTH, WIDTH, WIDTH, WIDTH,
            N_BRANCHES * D_MODEL)

VMEM_LIMIT = 56 * 1024 * 1024


def _const_spec(shape):
    return pl.BlockSpec(shape, lambda *_: (0,) * len(shape))


def _params(sem, vmem=VMEM_LIMIT):
    return pltpu.CompilerParams(dimension_semantics=sem, vmem_limit_bytes=vmem)


def _dot(a, b):
    return jnp.dot(a, b, preferred_element_type=F32)


def _dot_nt(a, b):
    return lax.dot_general(a, b, (((1,), (1,)), ((), ())), preferred_element_type=F32)


def _dot_tn(a, b):
    return lax.dot_general(a, b, (((0,), (0,)), ((), ())), preferred_element_type=F32)


def _rope_table_kernel(pos_ref, freq_ref, sgn_ref, cos_ref, sin_ref):
    ang = pos_ref[0].astype(F32) * freq_ref[...]
    act = sgn_ref[...] != 0.0
    cos_ref[0] = jnp.where(act, jnp.cos(ang), 1.0)
    sin_ref[0] = jnp.sin(ang) * sgn_ref[...]


def _rope_tables(positions, freq, sgn, ts=1024):
    B, S = positions.shape
    pos3 = positions.reshape(B, S, 1)
    tab = jax.ShapeDtypeStruct((B, S, 128), F32)
    return pl.pallas_call(
        _rope_table_kernel,
        out_shape=(tab, tab),
        grid=(B, S // ts),
        in_specs=[pl.BlockSpec((1, ts, 1), lambda b, s: (b, s, 0)),
                  _const_spec((1, 128)), _const_spec((1, 128))],
        out_specs=(pl.BlockSpec((1, ts, 128), lambda b, s: (b, s, 0)),
                   pl.BlockSpec((1, ts, 128), lambda b, s: (b, s, 0))),
        compiler_params=_params(("parallel", "parallel")),
        name="rope_tables",
    )(pos3, freq, sgn)


def _rope_lanes(x, cos, sin, lo_mask, half):
    swapped = jnp.where(lo_mask, pltpu.roll(x, 128 - half, 1), pltpu.roll(x, half, 1))
    return x * cos + swapped * sin


def _mla_proj_kernel(x_ref, cos_ref, sin_ref, wlat_ref, qn_ref, kvn_ref, wuq_ref, wuk_ref, wuv_ref,
                     q_ref, k_ref, v_ref):
    x = x_ref[0].astype(BF16)
    lat = _dot(x, wlat_ref[...])
    q_lat = lat[:, :MLA_Q_LORA]
    kv_lat = lat[:, MLA_Q_LORA:MLA_Q_LORA + MLA_KV_LORA]
    k_pe = lat[:, MLA_Q_LORA + MLA_KV_LORA:]

    def rms(v, g):
        return v * lax.rsqrt(jnp.mean(jnp.square(v), -1, keepdims=True) + RMS_EPS) * g

    cos = cos_ref[0]
    sin = sin_ref[0]
    lane = lax.broadcasted_iota(jnp.int32, cos.shape, 1)
    lo = lane < MLA_NOPE + MLA_ROPE // 2

    q = _dot(rms(q_lat, qn_ref[...]).astype(BF16), wuq_ref[...])
    kvn = rms(kv_lat, kvn_ref[...]).astype(BF16)
    k_nope = _dot(kvn, wuk_ref[...])
    v = _dot(kvn, wuv_ref[...])
    ones_lane = jnp.where(lane == MLA_V, 1.0, 0.0)
    k_rot = _rope_lanes(k_pe, cos, sin, lo, MLA_ROPE // 2)
    for h in range(HEADS):
        sl = slice(h * HEAD_PAD, (h + 1) * HEAD_PAD)
        qh = _rope_lanes(q[:, sl], cos, sin, lo, MLA_ROPE // 2)
        q_ref[0, :, sl] = (qh * (ATTN_SCALE * LOG2E)).astype(BF16)
        k_ref[0, :, sl] = (k_nope[:, sl] + k_rot).astype(BF16)
        v_ref[0, :, sl] = (v[:, sl] + ones_lane).astype(BF16)


def _mla_proj(x, cos, sin, wlat, qn, kvn, wuq, wuk, wuv, tm=512):
    B, S, _ = x.shape
    tok = lambda c: pl.BlockSpec((1, tm, c), lambda b, s: (b, s, 0))
    return pl.pallas_call(
        _mla_proj_kernel,
        out_shape=(jax.ShapeDtypeStruct((B, S, HEADS * HEAD_PAD), BF16),
                   jax.ShapeDtypeStruct((B, S, HEADS * HEAD_PAD), BF16),
                   jax.ShapeDtypeStruct((B, S, HEADS * HEAD_PAD), BF16)),
        grid=(B, S // tm),
        in_specs=[tok(D_MODEL), tok(128), tok(128),
                  _const_spec(wlat.shape), _const_spec(qn.shape), _const_spec(kvn.shape),
                  _const_spec(wuq.shape), _const_spec(wuk.shape), _const_spec(wuv.shape)],
        out_specs=(tok(HEADS * HEAD_PAD), tok(HEADS * HEAD_PAD), tok(HEADS * HEAD_PAD)),
        compiler_params=_params(("parallel", "parallel")),
        name="mla_proj",
    )(x, cos, sin, wlat, qn, kvn, wuq, wuk, wuv)


def _attn_kernel(q_ref, k_ref, v_ref, o_ref, m_s, acc_s, s_s, *, t):
    i = pl.program_id(2)
    m_s[...] = jnp.full(m_s.shape, NEG, F32)
    acc_s[...] = jnp.zeros(acc_s.shape, F32)
    heads = [slice(hh * HEAD_PAD, (hh + 1) * HEAD_PAD) for hh in range(2)]

    def scores(j, hh):
        start = pl.multiple_of(j * t, t)
        return _dot_nt(q_ref[0, :, heads[hh]], k_ref[0, pl.ds(start, t), heads[hh]])

    def accumulate(j, hh, s):
        start = pl.multiple_of(j * t, t)
        m = m_s[hh]
        m_new = jnp.maximum(m, jnp.max(s, -1, keepdims=True))
        p = jnp.concatenate([jnp.exp2(s[:, c * 128:(c + 1) * 128] - m_new) for c in range(t // 128)], axis=1)
        acc_s[hh] = jnp.exp2(m - m_new) * acc_s[hh] + _dot(p.astype(BF16), v_ref[0, pl.ds(start, t), heads[hh]])
        m_s[hh] = m_new

    for hh in range(2):
        s_s[hh] = scores(0, hh)

    def body(j, _):
        for hh in range(2):
            s_next = scores(j + 1, hh)
            accumulate(j, hh, s_s[hh])
            s_s[hh] = s_next
        return 0

    lax.fori_loop(0, i, body, 0)
    row = lax.broadcasted_iota(jnp.int32, (t, t), 0)
    col = lax.broadcasted_iota(jnp.int32, (t, t), 1)
    for hh in range(2):
        accumulate(i, hh, jnp.where(col <= row, s_s[hh], NEG))
    for hh in range(2):
        acc = acc_s[hh]
        o = acc[:, :MLA_V] / acc[:, MLA_V:MLA_V + 1]
        o_ref[0, :, hh * MLA_V:(hh + 1) * MLA_V] = o.astype(o_ref.dtype)


def _attention(q, k, v, t=512):
    B, S, _ = q.shape
    return pl.pallas_call(
        functools.partial(_attn_kernel, t=t),
        out_shape=jax.ShapeDtypeStruct((B, S, WIDTH), BF16),
        grid=(B, HEADS // 2, S // t),
        in_specs=[pl.BlockSpec((1, t, 2 * HEAD_PAD), lambda b, h, i: (b, i, h)),
                  pl.BlockSpec((1, S, 2 * HEAD_PAD), lambda b, h, i: (b, 0, h)),
                  pl.BlockSpec((1, S, 2 * HEAD_PAD), lambda b, h, i: (b, 0, h))],
        out_specs=pl.BlockSpec((1, t, 2 * MLA_V), lambda b, h, i: (b, i, h)),
        scratch_shapes=[pltpu.VMEM((2, t, HEAD_PAD), F32), pltpu.VMEM((2, t, HEAD_PAD), F32),
                        pltpu.VMEM((2, t, t), F32)],
        compiler_params=_params(("parallel", "parallel", "arbitrary")),
        name="mla_attention",
    )(q, k, v)


def _lru_kernel(x_ref, w_ref, cw_ref, cb_ref, wg_ref, bg_ref, lam_ref, y_ref, ubuf, hcar, h_s, *, ts):
    @pl.when(pl.program_id(1) == 0)
    def _():
        ubuf[0:8, :] = jnp.zeros((8, WIDTH), F32)
        hcar[...] = jnp.zeros_like(hcar)

    z = _dot(x_ref[0].astype(BF16), w_ref[...])
    u = z[:, :WIDTH]
    half_g = z[:, WIDTH:]
    ubuf[8:8 + ts, :] = u
    xc = cb_ref[...] + cw_ref[3:4, :] * u
    for j in range(1, LRU_CONV):
        xc = xc + cw_ref[3 - j:4 - j, :] * ubuf[8 - j:8 - j + ts, :]
    ubuf[0:8, :] = u[ts - 8:, :]

    th = jnp.tanh(_dot(xc.astype(BF16), wg_ref[...]) + bg_ref[...])
    lam = lam_ref[...]
    softplus_neg = jnp.maximum(-lam, 0.0) + jnp.log1p(jnp.exp(-jnp.abs(lam)))
    c1 = (-0.5 * LRU_C) * softplus_neg
    log_a = th[:, :WIDTH] * c1 + c1
    a = jnp.exp(log_a)
    t = jnp.tanh(log_a)
    b = jnp.sqrt(-t * (1.0 + a * a)) * ((th[:, WIDTH:] + 1.0) * (0.5 * xc))

    groups = ts // 8
    a3 = a.reshape(groups, 8, WIDTH)
    b3 = b.reshape(groups, 8, WIDTH)
    sub = lax.broadcasted_iota(jnp.int32, a3.shape, 1)
    for d in (1, 2, 4):
        keep = sub >= d
        b3 = a3 * jnp.where(keep, pltpu.roll(b3, d, 1), 0.0) + b3
        a3 = a3 * jnp.where(keep, pltpu.roll(a3, d, 1), 1.0)
    carry = hcar[...]
    for n in range(groups):
        h = a3[n] * carry + b3[n]
        h_s[n * 8:(n + 1) * 8, :] = h
        carry = jnp.broadcast_to(h[7:8, :], (8, WIDTH))
    hcar[...] = carry
    y_ref[0] = (h_s[...] * (half_g * jnp.tanh(half_g) + half_g)).astype(y_ref.dtype)


def _lru(x, w, cw, cb, wg, bg, lam, ts=512):
    B, S, _ = x.shape
    return pl.pallas_call(
        functools.partial(_lru_kernel, ts=ts),
        out_shape=jax.ShapeDtypeStruct((B, S, WIDTH), BF16),
        grid=(B, S // ts),
        in_specs=[pl.BlockSpec((1, ts, D_MODEL), lambda b, s: (b, s, 0)),
                  _const_spec(w.shape), _const_spec(cw.shape), _const_spec(cb.shape),
                  _const_spec(wg.shape), _const_spec(bg.shape), _const_spec(lam.shape)],
        out_specs=pl.BlockSpec((1, ts, WIDTH), lambda b, s: (b, s, 0)),
        scratch_shapes=[pltpu.VMEM((ts + 8, WIDTH), F32), pltpu.VMEM((8, WIDTH), F32),
                        pltpu.VMEM((ts, WIDTH), F32)],
        compiler_params=_params(("parallel", "arbitrary")),
        name="rglru",
    )(x, w, cw, cb, wg, bg, lam)


def _ret_kernel(x_ref, cos_ref, sin_ref, w_ref, decay_ref, qw_ref, kw_ref, cd_ref, gn_ref, avg_ref, y_ref,
                state, *, ts):
    C = RET_CHUNK
    n_chunks = ts // C

    @pl.when(pl.program_id(1) == 0)
    def _():
        state[...] = jnp.zeros_like(state)

    z = _dot(x_ref[0].astype(BF16), w_ref[...])
    cos = cos_ref[0]
    sin = sin_ref[0]
    lo = (lax.broadcasted_iota(jnp.int32, cos.shape, 1) % RET_DK) < RET_DK // 2
    first = lax.broadcasted_iota(jnp.int32, (C, 128), 1) < RET_DK
    same_head = (lax.broadcasted_iota(jnp.int32, (128, 128), 0) < RET_DK) == (
        lax.broadcasted_iota(jnp.int32, (128, 128), 1) < RET_DK)

    def group_mean(val):
        hi = val.astype(BF16)
        lo_part = (val - hi.astype(F32)).astype(BF16)
        return _dot(jnp.concatenate([hi, lo_part], axis=1), avg_ref[...])

    for p in range(HEADS // 2):
        ls = slice(p * 128, (p + 1) * 128)
        qb = _rope_lanes(z[:, ls], cos, sin, lo, RET_DK // 2)
        kb = _rope_lanes(z[:, WIDTH + p * 128:WIDTH + (p + 1) * 128], cos, sin, lo, RET_DK // 2)
        vb = z[:, 2 * WIDTH + p * 128:2 * WIDTH + (p + 1) * 128].astype(BF16)
        inner, kv, q_weighted = [], [], []
        for c in range(n_chunks):
            rs = slice(c * C, (c + 1) * C)
            qc, kc, vc = qb[rs], kb[rs], vb[rs]
            q2 = jnp.concatenate([jnp.where(first, qc, 0.0), jnp.where(first, 0.0, qc)], axis=0).astype(BF16)
            scores = _dot_nt(q2, kc.astype(BF16)) * decay_ref[p]
            inner2 = _dot(scores.astype(BF16), vc)
            inner.append(jnp.where(first, inner2[:C], inner2[C:]))
            kv.append(jnp.where(same_head, _dot_tn((kc * kw_ref[:, ls]).astype(BF16), vc), 0.0))
            q_weighted.append((qc * qw_ref[:, ls]).astype(BF16))
        s_prev = state[p]
        outs = []
        for c in range(n_chunks):
            outs.append(inner[c] + _dot(q_weighted[c], s_prev.astype(BF16)))
            s_prev = cd_ref[:, ls] * s_prev + kv[c]
        state[p] = s_prev
        o = jnp.concatenate(outs, axis=0)
        dev = o - group_mean(o)
        var = group_mean(dev * dev)
        half_g = z[:, 3 * WIDTH + p * 128:3 * WIDTH + (p + 1) * 128]
        y = dev * lax.rsqrt(var + LN_EPS) * gn_ref[:, ls] * (half_g * jnp.tanh(half_g) + half_g)
        y_ref[0, :, ls] = y.astype(y_ref.dtype)


def _retention(x, cos, sin, w, decay, qw, kw, cd, gn, avg, ts=512):
    B, S, _ = x.shape
    tok = lambda c: pl.BlockSpec((1, ts, c), lambda b, s: (b, s, 0))
    return pl.pallas_call(
        functools.partial(_ret_kernel, ts=ts),
        out_shape=jax.ShapeDtypeStruct((B, S, WIDTH), BF16),
        grid=(B, S // ts),
        in_specs=[tok(D_MODEL), tok(128), tok(128), _const_spec(w.shape), _const_spec(decay.shape),
                  _const_spec(qw.shape), _const_spec(kw.shape), _const_spec(cd.shape),
                  _const_spec(gn.shape), _const_spec(avg.shape)],
        out_specs=tok(WIDTH),
        scratch_shapes=[pltpu.VMEM((HEADS // 2, 128, 128), F32)],
        compiler_params=_params(("parallel", "arbitrary")),
        name="retention",
    )(x, cos, sin, w, decay, qw, kw, cd, gn, avg)


def _merge_kernel(x_ref, o_ref, yl_ref, yr_ref, wg_ref, bm_ref, wb_ref, wo_ref, lg_ref, lb_ref, out_ref):
    x = x_ref[0]
    zg = _dot(x.astype(BF16), wg_ref[...])
    half_g = zg[:, :WIDTH]
    y_mla = (o_ref[0].astype(F32) * (half_g * jnp.tanh(half_g) + half_g)).astype(BF16)
    ys = (y_mla, yl_ref[0], yr_ref[0])
    mixed = None
    for n in range(N_BRANCHES):
        lo = WIDTH + n * D_MODEL
        gate = 0.5 * jnp.tanh(zg[:, lo:lo + D_MODEL] + bm_ref[:, n * D_MODEL:(n + 1) * D_MODEL]) + 0.5
        term = gate * _dot(ys[n], wb_ref[n])
        mixed = term if mixed is None else mixed + term
    xf = ALPHA * x + _dot(mixed.astype(BF16), wo_ref[...])
    mu = jnp.mean(xf, -1, keepdims=True)
    var = jnp.mean(jnp.square(xf - mu), -1, keepdims=True)
    out_ref[0] = (xf - mu) * lax.rsqrt(var + LN_EPS) * lg_ref[...] + lb_ref[...]


def _merge(x, o, yl, yr, wg, bm, wb, wo, lg, lb, tm=256):
    B, S, _ = x.shape
    tok = lambda c: pl.BlockSpec((1, tm, c), lambda b, s: (b, s, 0))
    return pl.pallas_call(
        _merge_kernel,
        out_shape=jax.ShapeDtypeStruct((B, S, D_MODEL), F32),
        grid=(B, S // tm),
        in_specs=[tok(D_MODEL), tok(WIDTH), tok(WIDTH), tok(WIDTH),
                  _const_spec(wg.shape), _const_spec(bm.shape), _const_spec(wb.shape),
                  _const_spec(wo.shape), _const_spec(lg.shape), _const_spec(lb.shape)],
        out_specs=tok(D_MODEL),
        compiler_params=_params(("parallel", "parallel")),
        name="merge_out_ln",
    )(x, o, yl, yr, wg, bm, wb, wo, lg, lb)


def _rope_patterns():
    lane = jnp.arange(128)
    half = MLA_ROPE // 2
    inv = ROPE_BASE ** (-jnp.arange(half, dtype=F32) / half)
    in_rope = (lane >= MLA_NOPE) & (lane < MLA_NOPE + MLA_ROPE)
    idx = jnp.clip(lane - MLA_NOPE, 0, MLA_ROPE - 1) % half
    mla_freq = jnp.where(in_rope, inv[idx], 0.0).astype(F32)[None, :]
    mla_sgn = jnp.where(in_rope, jnp.where(lane < MLA_NOPE + half, -1.0, 1.0), 0.0).astype(F32)[None, :]
    half = RET_DK // 2
    inv = ROPE_BASE ** (-jnp.arange(half, dtype=F32) / half)
    ret_freq = inv[lane % half].astype(F32)[None, :]
    ret_sgn = jnp.where((lane % RET_DK) < half, -1.0, 1.0).astype(F32)[None, :]
    return mla_freq, mla_sgn, ret_freq, ret_sgn


def _retention_constants():
    C = RET_CHUNK
    log_gamma = jnp.log1p(-jnp.exp2(-5.0 - jnp.arange(HEADS, dtype=F32)))
    idx = jnp.arange(C, dtype=F32)
    diff = idx[:, None] - idx[None, :]
    decay = jnp.where(diff[None] >= 0, jnp.exp(diff[None] * log_gamma[:, None, None]), 0.0)
    k_w = jnp.exp((C - 1.0 - idx)[:, None] * log_gamma[None, :])
    q_w = jnp.exp((idx + 1.0)[:, None] * log_gamma[None, :])
    chunk_decay = jnp.exp(C * log_gamma)
    expand = lambda t: jnp.repeat(t, RET_DK, axis=-1)
    lane_head = jnp.arange(128) // RET_DK
    avg = jnp.where(lane_head[:, None] == lane_head[None, :], 1.0 / RET_DK, 0.0)
    avg = jnp.concatenate([avg, avg], axis=0).astype(BF16)
    return (decay.reshape(HEADS // 2, 2 * C, C), expand(q_w), expand(k_w), expand(chunk_decay[None, :]), avg)


def _block_diag(w):
    n, a, b = w.shape
    return (jnp.eye(n, dtype=w.dtype)[:, None, :, None] * w[:, :, None, :]).reshape(n * a, n * b)


def kernel(x, positions, w_in, b_merge, mla_q_norm, mla_w_uq, mla_kv_norm, mla_w_ukv, lru_conv_w, lru_conv_b,
           lru_w_r, lru_b_r, lru_w_i, lru_b_i, lru_lambda, ret_gn_g, w_branch, w_out, ln_g, ln_b):
    mla_freq, mla_sgn, ret_freq, ret_sgn = _rope_patterns()
    mla_cos, mla_sin = _rope_tables(positions, mla_freq, mla_sgn)
    ret_cos, ret_sin = _rope_tables(positions, ret_freq, ret_sgn)
    decay, q_w, k_w, chunk_decay, avg = _retention_constants()

    offs = [0]
    for s in IN_SIZES:
        offs.append(offs[-1] + s)

    for l in range(DEPTH):
        w = w_in[l]
        col = lambda i, j=None: w[:, offs[i]:offs[(i if j is None else j) + 1]]
        zeros = lambda n: jnp.zeros((D_MODEL, n), F32)
        w_lat = jnp.concatenate([col(0), col(1), zeros(MLA_NOPE), col(2), zeros(HEAD_PAD - MLA_NOPE - MLA_ROPE)],
                                axis=1).astype(BF16)
        w_uq = jnp.pad(mla_w_uq[l].reshape(MLA_Q_LORA, HEADS, MLA_NOPE + MLA_ROPE),
                       ((0, 0), (0, 0), (0, HEAD_PAD - MLA_NOPE - MLA_ROPE)))
        w_uq = w_uq.reshape(MLA_Q_LORA, HEADS * HEAD_PAD).astype(BF16)
        w_ukv = mla_w_ukv[l].reshape(MLA_KV_LORA, HEADS, MLA_NOPE + MLA_V)
        w_uk = jnp.pad(w_ukv[:, :, :MLA_NOPE], ((0, 0), (0, 0), (0, HEAD_PAD - MLA_NOPE)))
        w_uk = w_uk.reshape(MLA_KV_LORA, HEADS * HEAD_PAD).astype(BF16)
        w_uv = jnp.pad(w_ukv[:, :, MLA_NOPE:], ((0, 0), (0, 0), (0, HEAD_PAD - MLA_V)))
        w_uv = w_uv.reshape(MLA_KV_LORA, HEADS * HEAD_PAD).astype(BF16)

        q, k, v = _mla_proj(x, mla_cos, mla_sin, w_lat, mla_q_norm[l][None, :], mla_kv_norm[l][None, :],
                            w_uq, w_uk, w_uv)
        o_mla = _attention(q, k, v)

        w_gates = (0.5 * jnp.concatenate([_block_diag(lru_w_r[l]), _block_diag(lru_w_i[l])], axis=1)).astype(BF16)
        b_gates = 0.5 * jnp.concatenate([lru_b_r[l], lru_b_i[l]])[None, :]
        w_lru = jnp.concatenate([col(4), 0.5 * col(5)], axis=1).astype(BF16)
        y_lru = _lru(x, w_lru, lru_conv_w[l], lru_conv_b[l][None, :], w_gates, b_gates, lru_lambda[l][None, :])

        w_ret = jnp.concatenate([col(6), RET_DK ** -0.5 * col(7), col(8), 0.5 * col(9)], axis=1).astype(BF16)
        y_ret = _retention(x, ret_cos, ret_sin, w_ret, decay, q_w, k_w, chunk_decay, ret_gn_g[l][None, :], avg)

        w_gm = (0.5 * jnp.concatenate([col(3), col(10)], axis=1)).astype(BF16)
        x = _merge(x, o_mla, y_lru, y_ret, w_gm, 0.5 * b_merge[l][None, :], w_branch[l].astype(BF16),
                   w_out[l].astype(BF16), ln_g[l][None, :], ln_b[l][None, :])
    return x
```

```python
import functools

import jax
import jax.numpy as jnp
from jax import lax
from jax.experimental import pallas as pl
from jax.experimental.pallas import tpu as pltpu

F32 = jnp.float32
BF16 = jnp.bfloat16

D_MODEL = 1024
DEPTH = 2
HEADS = 8
MLA_Q_LORA = 384
MLA_KV_LORA = 256
MLA_NOPE = 64
MLA_ROPE = 32
MLA_V = 64
HEAD_PAD = 128
WIDTH = 512
LRU_BLOCKS = 8
LRU_CONV = 4
LRU_C = 8.0
LRU_ROWS = 256
MERGE_ROWS = 256
RET_DK = 64
RET_CHUNK = 128
N_BRANCHES = 3
ROPE_BASE = 10000.0
LN_EPS = 1e-5
RMS_EPS = 1e-6
ALPHA = (2 * DEPTH) ** 0.25
ATTN_SCALE = (MLA_NOPE + MLA_ROPE) ** -0.5
NEG = -1e30
LOG2E = 1.4426950408889634

IN_SIZES = (MLA_Q_LORA, MLA_KV_LORA, MLA_ROPE, WIDTH, WIDTH, WIDTH, WIDTH, WIDTH, WIDTH, WIDTH,
            N_BRANCHES * D_MODEL)

VMEM_LIMIT = 56 * 1024 * 1024


def _const_spec(shape):
    return pl.BlockSpec(shape, lambda *_: (0,) * len(shape))


def _params(sem, vmem=VMEM_LIMIT):
    return pltpu.CompilerParams(dimension_semantics=sem, vmem_limit_bytes=vmem)


def _dot(a, b):
    return jnp.dot(a, b, preferred_element_type=F32)


def _dot_nt(a, b):
    return lax.dot_general(a, b, (((1,), (1,)), ((), ())), preferred_element_type=F32)


def _dot_tn(a, b):
    return lax.dot_general(a, b, (((0,), (0,)), ((), ())), preferred_element_type=F32)


def _rope_table_kernel(pos_ref, freq_ref, sgn_ref, base_ref, cos_ref, sin_ref):
    ang = pos_ref[0].astype(F32) * freq_ref[...]
    act = sgn_ref[...] != 0.0
    cos_ref[0] = jnp.where(act, jnp.cos(ang), base_ref[...])
    sin_ref[0] = jnp.sin(ang) * sgn_ref[...]


def _rope_tables(positions, freq, sgn, base, ts=1024):
    B, S = positions.shape
    pos3 = positions.reshape(B, S, 1)
    tab = jax.ShapeDtypeStruct((B, S, 128), F32)
    return pl.pallas_call(
        _rope_table_kernel,
        out_shape=(tab, tab),
        grid=(B, S // ts),
        in_specs=[pl.BlockSpec((1, ts, 1), lambda b, s: (b, s, 0)),
                  _const_spec((1, 128)), _const_spec((1, 128)), _const_spec((1, 128))],
        out_specs=(pl.BlockSpec((1, ts, 128), lambda b, s: (b, s, 0)),
                   pl.BlockSpec((1, ts, 128), lambda b, s: (b, s, 0))),
        compiler_params=_params(("parallel", "parallel")),
        name="rope_tables",
    )(pos3, freq, sgn, base)


def _rope_lanes(x, cos, sin, lo_mask, half):
    swapped = jnp.where(lo_mask, pltpu.roll(x, 128 - half, 1), pltpu.roll(x, half, 1))
    return x * cos + swapped * sin


def _mla_proj_kernel(x_ref, cos_ref, sin_ref, wlat_ref, qn_ref, kvn_ref, wuq_ref, wuk_ref, wuv_ref,
                     q_ref, k_ref, v_ref):
    x = x_ref[0].astype(BF16)
    lat = _dot(x, wlat_ref[...])
    q_lat = lat[:, :MLA_Q_LORA]
    kv_lat = lat[:, MLA_Q_LORA:MLA_Q_LORA + MLA_KV_LORA]
    k_pe = lat[:, MLA_Q_LORA + MLA_KV_LORA:]

    def rms(v, g):
        return v * lax.rsqrt(jnp.mean(jnp.square(v), -1, keepdims=True) + RMS_EPS) * g

    def rotate(t, c, s):
        return t * c + pltpu.roll(t, 128 - MLA_ROPE // 2, 1) * s

    cos = cos_ref[0]
    sin = sin_ref[0]
    cos_q = cos * (ATTN_SCALE * LOG2E)
    sin_q = sin * (ATTN_SCALE * LOG2E)
    lane = lax.broadcasted_iota(jnp.int32, cos.shape, 1)

    q = _dot(rms(q_lat, qn_ref[...]).astype(BF16), wuq_ref[...])
    kvn = rms(kv_lat, kvn_ref[...]).astype(BF16)
    k_nope = _dot(kvn, wuk_ref[...])
    v = _dot(kvn, wuv_ref[...])
    ones_lane = jnp.where(lane == MLA_V, 1.0, 0.0)
    k_rot = rotate(k_pe, cos, sin)
    for h in range(HEADS):
        sl = slice(h * HEAD_PAD, (h + 1) * HEAD_PAD)
        q_ref[0, :, sl] = rotate(q[:, sl], cos_q, sin_q).astype(BF16)
        k_ref[0, :, sl] = (k_nope[:, sl] + k_rot).astype(BF16)
        v_ref[0, :, sl] = (v[:, sl] + ones_lane).astype(BF16)


def _mla_proj(x, cos, sin, wlat, qn, kvn, wuq, wuk, wuv, tm=512):
    B, S, _ = x.shape
    tok = lambda c: pl.BlockSpec((1, tm, c), lambda b, s: (b, s, 0))
    return pl.pallas_call(
        _mla_proj_kernel,
        out_shape=(jax.ShapeDtypeStruct((B, S, HEADS * HEAD_PAD), BF16),
                   jax.ShapeDtypeStruct((B, S, HEADS * HEAD_PAD), BF16),
                   jax.ShapeDtypeStruct((B, S, HEADS * HEAD_PAD), BF16)),
        grid=(B, S // tm),
        in_specs=[tok(D_MODEL), tok(128), tok(128),
                  _const_spec(wlat.shape), _const_spec(qn.shape), _const_spec(kvn.shape),
                  _const_spec(wuq.shape), _const_spec(wuk.shape), _const_spec(wuv.shape)],
        out_specs=(tok(HEADS * HEAD_PAD), tok(HEADS * HEAD_PAD), tok(HEADS * HEAD_PAD)),
        compiler_params=_params(("parallel", "parallel")),
        name="mla_proj",
    )(x, cos, sin, wlat, qn, kvn, wuq, wuk, wuv)


def _attn_kernel(q_ref, k_ref, v_ref, o_ref, m_s, acc_s, s_a, s_b, *, tq, tk):
    i = pl.program_id(2)
    ratio = tq // tk
    assert ratio == 2
    m_s[...] = jnp.full(m_s.shape, NEG, F32)
    acc_s[...] = jnp.zeros(acc_s.shape, F32)
    heads = [slice(hh * HEAD_PAD, (hh + 1) * HEAD_PAD) for hh in range(2)]

    top, bottom = slice(0, tk), slice(tk, tq)

    def scores(j, hh, rows):
        start = pl.multiple_of(j * tk, tk)
        return _dot_nt(q_ref[0, rows, heads[hh]], k_ref[0, pl.ds(start, tk), heads[hh]])

    def accumulate(j, hh, s, rows):
        start = pl.multiple_of(j * tk, tk)
        m = m_s[hh, rows, :]
        m_new = jnp.maximum(m, jnp.max(s, -1, keepdims=True))
        p = jnp.concatenate([jnp.exp2(s[:, c * 128:(c + 1) * 128] - m_new) for c in range(tk // 128)], axis=1)
        pv = _dot(p.astype(BF16), v_ref[0, pl.ds(start, tk), heads[hh]])
        acc_s[hh, rows, :] = jnp.exp2(m - m_new) * acc_s[hh, rows, :] + pv
        m_s[hh, rows, :] = m_new

    for rows in (top, bottom):
        for hh in range(2):
            s_a[hh, rows, :] = scores(0, hh, rows)

    def body(n, _):
        for cur, nxt, j in ((s_a, s_b, 2 * n), (s_b, s_a, 2 * n + 1)):
            for rows in (top, bottom):
                for hh in range(2):
                    nxt[hh, rows, :] = scores(j + 1, hh, rows)
                    accumulate(j, hh, cur[hh, rows, :], rows)
        return 0

    lax.fori_loop(0, i, body, 0)
    causal = (lax.broadcasted_iota(jnp.int32, (tk, tk), 1) <= lax.broadcasted_iota(jnp.int32, (tk, tk), 0))
    for hh in range(2):
        s_b[hh, bottom, :] = scores(2 * i + 1, hh, bottom)
        accumulate(2 * i, hh, jnp.where(causal, s_a[hh, top, :], NEG), top)
        accumulate(2 * i, hh, s_a[hh, bottom, :], bottom)
        accumulate(2 * i + 1, hh, jnp.where(causal, s_b[hh, bottom, :], NEG), bottom)
    for hh in range(2):
        acc = acc_s[hh]
        o = acc[:, :MLA_V] / acc[:, MLA_V:MLA_V + 1]
        o_ref[0, :, hh * MLA_V:(hh + 1) * MLA_V] = o.astype(o_ref.dtype)


def _attention(q, k, v, tq=1024, tk=512):
    B, S, _ = q.shape
    return pl.pallas_call(
        functools.partial(_attn_kernel, tq=tq, tk=tk),
        out_shape=jax.ShapeDtypeStruct((B, S, WIDTH), BF16),
        grid=(B, HEADS // 2, S // tq),
        in_specs=[pl.BlockSpec((1, tq, 2 * HEAD_PAD), lambda b, h, i: (b, i, h)),
                  pl.BlockSpec((1, S, 2 * HEAD_PAD), lambda b, h, i: (b, 0, h)),
                  pl.BlockSpec((1, S, 2 * HEAD_PAD), lambda b, h, i: (b, 0, h))],
        out_specs=pl.BlockSpec((1, tq, 2 * MLA_V), lambda b, h, i: (b, i, h)),
        scratch_shapes=[pltpu.VMEM((2, tq, HEAD_PAD), F32), pltpu.VMEM((2, tq, HEAD_PAD), F32),
                        pltpu.VMEM((2, tq, tk), F32), pltpu.VMEM((2, tq, tk), F32)],
        compiler_params=_params(("parallel", "parallel", "arbitrary")),
        name="mla_attention",
    )(q, k, v)


def _lru_kernel(x_ref, w_ref, cw_ref, cb_ref, wg_ref, bg_ref, lam_ref, y_ref, ubuf, hcar, h_s, *, ts):
    @pl.when(pl.program_id(1) == 0)
    def _():
        ubuf[0:8, :] = jnp.zeros((8, WIDTH), F32)
        hcar[...] = jnp.zeros_like(hcar)

    nb = ts // LRU_ROWS
    blocks = [slice(r * LRU_ROWS, (r + 1) * LRU_ROWS) for r in range(nb)]
    z = [_dot(x_ref[0, rs, :].astype(BF16), w_ref[...]) for rs in blocks]
    xc = []
    for r, rs in enumerate(blocks):
        u = z[r][:, :WIDTH]
        lo = 8 + r * LRU_ROWS
        ubuf[lo:lo + LRU_ROWS, :] = u
        acc = cb_ref[...] + cw_ref[3:4, :] * u
        for j in range(1, LRU_CONV):
            acc = acc + cw_ref[3 - j:4 - j, :] * ubuf[lo - j:lo - j + LRU_ROWS, :]
        xc.append(acc)
    ubuf[0:8, :] = z[nb - 1][LRU_ROWS - 8:, :WIDTH]

    th = [jnp.tanh(_dot(xc[r].astype(BF16), wg_ref[...]) + bg_ref[...]) for r in range(nb)]
    lam = lam_ref[...]
    softplus_neg = jnp.maximum(-lam, 0.0) + jnp.log1p(jnp.exp(-jnp.abs(lam)))
    c1 = (-0.5 * LRU_C) * softplus_neg
    groups = LRU_ROWS // 8
    sub = lax.broadcasted_iota(jnp.int32, (groups, 8, WIDTH), 1)
    a3, b3 = [], []
    for r in range(nb):
        log_a = th[r][:, :WIDTH] * c1 + c1
        a = jnp.exp(log_a)
        t = jnp.tanh(log_a)
        b_sq = -t * (1.0 + a * a)
        root = jnp.where(b_sq > 0.0, b_sq * lax.rsqrt(b_sq), 0.0)
        b = root * ((th[r][:, WIDTH:] + 1.0) * (0.5 * xc[r]))
        a = a.reshape(groups, 8, WIDTH)
        b = b.reshape(groups, 8, WIDTH)
        for d in (1, 2, 4):
            keep = sub >= d
            b = a * jnp.where(keep, pltpu.roll(b, d, 1), 0.0) + b
            a = a * jnp.where(keep, pltpu.roll(a, d, 1), 1.0)
        a3.append(a)
        b3.append(b)
    carry = hcar[...]
    for r in range(nb):
        for n in range(groups):
            h = a3[r][n] * carry + b3[r][n]
            h_s[r * LRU_ROWS + n * 8:r * LRU_ROWS + (n + 1) * 8, :] = h
            carry = jnp.broadcast_to(h[7:8, :], (8, WIDTH))
    hcar[...] = carry
    for r, rs in enumerate(blocks):
        half_g = z[r][:, WIDTH:]
        y_ref[0, rs, :] = (h_s[rs, :] * (half_g * jnp.tanh(half_g) + half_g)).astype(y_ref.dtype)


def _lru(x, w, cw, cb, wg, bg, lam, ts=512):
    B, S, _ = x.shape
    return pl.pallas_call(
        functools.partial(_lru_kernel, ts=ts),
        out_shape=jax.ShapeDtypeStruct((B, S, WIDTH), BF16),
        grid=(B, S // ts),
        in_specs=[pl.BlockSpec((1, ts, D_MODEL), lambda b, s: (b, s, 0)),
                  _const_spec(w.shape), _const_spec(cw.shape), _const_spec(cb.shape),
                  _const_spec(wg.shape), _const_spec(bg.shape), _const_spec(lam.shape)],
        out_specs=pl.BlockSpec((1, ts, WIDTH), lambda b, s: (b, s, 0)),
        scratch_shapes=[pltpu.VMEM((ts + 8, WIDTH), F32), pltpu.VMEM((8, WIDTH), F32),
                        pltpu.VMEM((ts, WIDTH), F32)],
        compiler_params=_params(("parallel", "arbitrary")),
        name="rglru",
    )(x, w, cw, cb, wg, bg, lam)


def _ret_kernel(x_ref, cos_ref, sin_ref, w_ref, decay_ref, qw_ref, kw_ref, cd_ref, gn_ref, avg_ref, y_ref,
                state, *, ts):
    C = RET_CHUNK
    n_chunks = ts // C

    @pl.when(pl.program_id(1) == 0)
    def _():
        state[...] = jnp.zeros_like(state)

    xb = x_ref[0].astype(BF16)
    cos = cos_ref[0]
    sin = sin_ref[0]
    lo = (lax.broadcasted_iota(jnp.int32, cos.shape, 1) % RET_DK) < RET_DK // 2
    first = lax.broadcasted_iota(jnp.int32, (C, 128), 1) < RET_DK
    same_head = (lax.broadcasted_iota(jnp.int32, (128, 128), 0) < RET_DK) == (
        lax.broadcasted_iota(jnp.int32, (128, 128), 1) < RET_DK)

    def group_mean(val):
        hi = val.astype(BF16)
        lo_part = (val - hi.astype(F32)).astype(BF16)
        return _dot(jnp.concatenate([hi, lo_part], axis=1), avg_ref[...])

    pairs = range(HEADS // 2)
    chunks = range(n_chunks)
    tiles = [(p, c) for p in pairs for c in chunks]
    lanes = [slice(p * 128, (p + 1) * 128) for p in pairs]
    z = _dot(xb, w_ref[...])
    qb = [_rope_lanes(z[:, p * 512:p * 512 + 128], cos, sin, lo, RET_DK // 2) for p in pairs]
    kb = [_rope_lanes(z[:, p * 512 + 128:p * 512 + 256], cos, sin, lo, RET_DK // 2) for p in pairs]
    vb = [z[:, p * 512 + 256:p * 512 + 384].astype(BF16) for p in pairs]
    rows = [slice(c * C, (c + 1) * C) for c in chunks]
    q2 = {(p, c): jnp.concatenate([jnp.where(first, qb[p][rows[c]], 0.0), jnp.where(first, 0.0, qb[p][rows[c]])],
                                  axis=0).astype(BF16) for p, c in tiles}
    scores = {(p, c): _dot_nt(q2[p, c], kb[p][rows[c]].astype(BF16)) * decay_ref[p] for p, c in tiles}
    inner2 = {(p, c): _dot(scores[p, c].astype(BF16), vb[p][rows[c]]) for p, c in tiles}
    inner = {t: jnp.where(first, inner2[t][:C], inner2[t][C:]) for t in tiles}
    kv = {(p, c): jnp.where(same_head, _dot_tn((kb[p][rows[c]] * kw_ref[:, lanes[p]]).astype(BF16), vb[p][rows[c]]),
                            0.0) for p, c in tiles}
    q_weighted = {(p, c): (qb[p][rows[c]] * qw_ref[:, lanes[p]]).astype(BF16) for p, c in tiles}
    s_prev = [state[p] for p in pairs]
    outs = {}
    for c in chunks:
        for p in pairs:
            outs[p, c] = inner[p, c] + _dot(q_weighted[p, c], s_prev[p].astype(BF16))
            s_prev[p] = cd_ref[:, lanes[p]] * s_prev[p] + kv[p, c]
    for p in pairs:
        state[p] = s_prev[p]
    o = [jnp.concatenate([outs[p, c] for c in chunks], axis=0) for p in pairs]
    mean = [group_mean(o[p]) for p in pairs]
    dev = [o[p] - mean[p] for p in pairs]
    var = [group_mean(dev[p] * dev[p]) for p in pairs]
    for p in pairs:
        half_g = z[:, p * 512 + 384:(p + 1) * 512]
        y = dev[p] * lax.rsqrt(var[p] + LN_EPS) * gn_ref[:, lanes[p]] * (half_g * jnp.tanh(half_g) + half_g)
        y_ref[0, :, lanes[p]] = y.astype(y_ref.dtype)


def _retention(x, cos, sin, w, decay, qw, kw, cd, gn, avg, ts=512):
    B, S, _ = x.shape
    tok = lambda c: pl.BlockSpec((1, ts, c), lambda b, s: (b, s, 0))
    return pl.pallas_call(
        functools.partial(_ret_kernel, ts=ts),
        out_shape=jax.ShapeDtypeStruct((B, S, WIDTH), BF16),
        grid=(B, S // ts),
        in_specs=[tok(D_MODEL), tok(128), tok(128), _const_spec(w.shape), _const_spec(decay.shape),
                  _const_spec(qw.shape), _const_spec(kw.shape), _const_spec(cd.shape),
                  _const_spec(gn.shape), _const_spec(avg.shape)],
        out_specs=tok(WIDTH),
        scratch_shapes=[pltpu.VMEM((HEADS // 2, 128, 128), F32)],
        compiler_params=_params(("parallel", "arbitrary")),
        name="retention",
    )(x, cos, sin, w, decay, qw, kw, cd, gn, avg)


def _merge_kernel(x_ref, o_ref, yl_ref, yr_ref, wg_ref, bm_ref, wb_ref, wo_ref, lg_ref, lb_ref, out_ref):
    tm = x_ref.shape[1]
    blocks = [slice(r, r + MERGE_ROWS) for r in range(0, tm, MERGE_ROWS)]
    zg = [_dot(x_ref[0, rs, :].astype(BF16), wg_ref[...]) for rs in blocks]
    mixed = []
    for r, rs in enumerate(blocks):
        half_g = zg[r][:, :WIDTH]
        y_mla = (o_ref[0, rs, :].astype(F32) * (half_g * jnp.tanh(half_g) + half_g)).astype(BF16)
        ys = (y_mla, yl_ref[0, rs, :], yr_ref[0, rs, :])
        total = None
        for n in range(N_BRANCHES):
            lo = WIDTH + n * D_MODEL
            gate = 0.5 * jnp.tanh(zg[r][:, lo:lo + D_MODEL] + bm_ref[:, n * D_MODEL:(n + 1) * D_MODEL]) + 0.5
            term = gate * _dot(ys[n], wb_ref[n])
            total = term if total is None else total + term
        mixed.append(total.astype(BF16))
    proj = [_dot(mixed[r], wo_ref[...]) for r in range(len(blocks))]
    for r, rs in enumerate(blocks):
        xf = ALPHA * x_ref[0, rs, :] + proj[r]
        mu = jnp.mean(xf, -1, keepdims=True)
        var = jnp.mean(jnp.square(xf - mu), -1, keepdims=True)
        out_ref[0, rs, :] = (xf - mu) * lax.rsqrt(var + LN_EPS) * lg_ref[...] + lb_ref[...]


def _merge(x, o, yl, yr, wg, bm, wb, wo, lg, lb, tm=512):
    B, S, _ = x.shape
    tok = lambda c: pl.BlockSpec((1, tm, c), lambda b, s: (b, s, 0))
    return pl.pallas_call(
        _merge_kernel,
        out_shape=jax.ShapeDtypeStruct((B, S, D_MODEL), F32),
        grid=(B, S // tm),
        in_specs=[tok(D_MODEL), tok(WIDTH), tok(WIDTH), tok(WIDTH),
                  _const_spec(wg.shape), _const_spec(bm.shape), _const_spec(wb.shape),
                  _const_spec(wo.shape), _const_spec(lg.shape), _const_spec(lb.shape)],
        out_specs=tok(D_MODEL),
        compiler_params=_params(("parallel", "parallel")),
        name="merge_out_ln",
    )(x, o, yl, yr, wg, bm, wb, wo, lg, lb)


def _rope_patterns():
    lane = jnp.arange(128)
    half = MLA_ROPE // 2
    inv = ROPE_BASE ** (-jnp.arange(half, dtype=F32) / half)
    in_rope = (lane >= MLA_NOPE) & (lane < MLA_NOPE + MLA_ROPE)
    idx = jnp.clip(lane - MLA_NOPE, 0, MLA_ROPE - 1) % half
    mla_freq = jnp.where(in_rope, inv[idx], 0.0).astype(F32)[None, :]
    mla_sgn = jnp.where(in_rope, jnp.where(lane < MLA_NOPE + half, -1.0, 1.0), 0.0).astype(F32)[None, :]
    mla_base = jnp.where(lane < MLA_NOPE, 1.0, 0.0).astype(F32)[None, :]
    half = RET_DK // 2
    inv = ROPE_BASE ** (-jnp.arange(half, dtype=F32) / half)
    ret_freq = inv[lane % half].astype(F32)[None, :]
    ret_sgn = jnp.where((lane % RET_DK) < half, -1.0, 1.0).astype(F32)[None, :]
    return (mla_freq, mla_sgn, mla_base), (ret_freq, ret_sgn, jnp.ones((1, 128), F32))


def _retention_constants():
    C = RET_CHUNK
    log_gamma = jnp.log1p(-jnp.exp2(-5.0 - jnp.arange(HEADS, dtype=F32)))
    idx = jnp.arange(C, dtype=F32)
    diff = idx[:, None] - idx[None, :]
    decay = jnp.where(diff[None] >= 0, jnp.exp(diff[None] * log_gamma[:, None, None]), 0.0)
    k_w = jnp.exp((C - 1.0 - idx)[:, None] * log_gamma[None, :])
    q_w = jnp.exp((idx + 1.0)[:, None] * log_gamma[None, :])
    chunk_decay = jnp.exp(C * log_gamma)
    expand = lambda t: jnp.repeat(t, RET_DK, axis=-1)
    lane_head = jnp.arange(128) // RET_DK
    avg = jnp.where(lane_head[:, None] == lane_head[None, :], 1.0 / RET_DK, 0.0)
    avg = jnp.concatenate([avg, avg], axis=0).astype(BF16)
    return (decay.reshape(HEADS // 2, 2 * C, C), expand(q_w), expand(k_w), expand(chunk_decay[None, :]), avg)


def _block_diag(w):
    n, a, b = w.shape
    return (jnp.eye(n, dtype=w.dtype)[:, None, :, None] * w[:, :, None, :]).reshape(n * a, n * b)


def kernel(x, positions, w_in, b_merge, mla_q_norm, mla_w_uq, mla_kv_norm, mla_w_ukv, lru_conv_w, lru_conv_b,
           lru_w_r, lru_b_r, lru_w_i, lru_b_i, lru_lambda, ret_gn_g, w_branch, w_out, ln_g, ln_b):
    mla_pattern, ret_pattern = _rope_patterns()
    mla_cos, mla_sin = _rope_tables(positions, *mla_pattern)
    ret_cos, ret_sin = _rope_tables(positions, *ret_pattern)
    decay, q_w, k_w, chunk_decay, avg = _retention_constants()

    offs = [0]
    for s in IN_SIZES:
        offs.append(offs[-1] + s)

    for l in range(DEPTH):
        w = w_in[l]
        col = lambda i, j=None: w[:, offs[i]:offs[(i if j is None else j) + 1]]
        zeros = lambda n: jnp.zeros((D_MODEL, n), F32)
        half = MLA_ROPE // 2
        w_lat = jnp.concatenate([col(0), col(1), zeros(MLA_NOPE), col(2), col(2)[:, :half],
                                 zeros(HEAD_PAD - MLA_NOPE - MLA_ROPE - half)], axis=1).astype(BF16)
        w_uq = mla_w_uq[l].reshape(MLA_Q_LORA, HEADS, MLA_NOPE + MLA_ROPE)
        w_uq = jnp.concatenate([w_uq, w_uq[:, :, MLA_NOPE:MLA_NOPE + half],
                                jnp.zeros((MLA_Q_LORA, HEADS, HEAD_PAD - MLA_NOPE - MLA_ROPE - half), F32)], axis=2)
        w_uq = w_uq.reshape(MLA_Q_LORA, HEADS * HEAD_PAD).astype(BF16)
        w_ukv = mla_w_ukv[l].reshape(MLA_KV_LORA, HEADS, MLA_NOPE + MLA_V)
        w_uk = jnp.pad(w_ukv[:, :, :MLA_NOPE], ((0, 0), (0, 0), (0, HEAD_PAD - MLA_NOPE)))
        w_uk = w_uk.reshape(MLA_KV_LORA, HEADS * HEAD_PAD).astype(BF16)
        w_uv = jnp.pad(w_ukv[:, :, MLA_NOPE:], ((0, 0), (0, 0), (0, HEAD_PAD - MLA_V)))
        w_uv = w_uv.reshape(MLA_KV_LORA, HEADS * HEAD_PAD).astype(BF16)

        q, k, v = _mla_proj(x, mla_cos, mla_sin, w_lat, mla_q_norm[l][None, :], mla_kv_norm[l][None, :],
                            w_uq, w_uk, w_uv)
        o_mla = _attention(q, k, v)

        w_gates = (0.5 * jnp.concatenate([_block_diag(lru_w_r[l]), _block_diag(lru_w_i[l])], axis=1)).astype(BF16)
        b_gates = 0.5 * jnp.concatenate([lru_b_r[l], lru_b_i[l]])[None, :]
        w_lru = jnp.concatenate([col(4), 0.5 * col(5)], axis=1).astype(BF16)
        y_lru = _lru(x, w_lru, lru_conv_w[l], lru_conv_b[l][None, :], w_gates, b_gates, lru_lambda[l][None, :])

        w_ret = jnp.stack([col(6), RET_DK ** -0.5 * col(7), col(8), 0.5 * col(9)], axis=1)
        w_ret = w_ret.reshape(D_MODEL, 4, HEADS // 2, 128).transpose(0, 2, 1, 3)
        w_ret = w_ret.reshape(D_MODEL, 4 * WIDTH).astype(BF16)
        y_ret = _retention(x, ret_cos, ret_sin, w_ret, decay, q_w, k_w, chunk_decay, ret_gn_g[l][None, :], avg)

        w_gm = (0.5 * jnp.concatenate([col(3), col(10)], axis=1)).astype(BF16)
        x = _merge(x, o_mla, y_lru, y_ret, w_gm, 0.5 * b_merge[l][None, :], w_branch[l].astype(BF16),
                   w_out[l].astype(BF16), ln_g[l][None, :], ln_b[l][None, :])
    return x
```

```python
import functools

import jax
import jax.numpy as jnp
from jax import lax
from jax.experimental import pallas as pl
from jax.experimental.pallas import tpu as pltpu

F32 = jnp.float32
BF16 = jnp.bfloat16

D_MODEL = 1024
DEPTH = 2
HEADS = 8
MLA_Q_LORA = 384
MLA_KV_LORA = 256
MLA_NOPE = 64
MLA_ROPE = 32
MLA_V = 64
HEAD_PAD = 128
WIDTH = 512
LRU_CONV = 4
LRU_C = 8.0
LRU_ROWS = 256
MERGE_ROWS = 256
PROJ_ROWS = 256
RET_DK = 64
RET_CHUNK = 128
N_BRANCHES = 3
ROPE_BASE = 10000.0
LN_EPS = 1e-5
RMS_EPS = 1e-6
ALPHA = (2 * DEPTH) ** 0.25
ATTN_SCALE = (MLA_NOPE + MLA_ROPE) ** -0.5
NEG = -1e30
LOG2E = 1.4426950408889634

IN_SIZES = (MLA_Q_LORA, MLA_KV_LORA, MLA_ROPE, WIDTH, WIDTH, WIDTH, WIDTH, WIDTH, WIDTH, WIDTH,
            N_BRANCHES * D_MODEL)
LAT_COLS = MLA_Q_LORA + MLA_KV_LORA + HEAD_PAD

VMEM_LIMIT = 56 * 1024 * 1024


def _const_spec(shape):
    return pl.BlockSpec(shape, lambda *_: (0,) * len(shape))


def _layer_spec(arr, layer, width=None, block=0):
    shape = arr.shape[1:]
    if width is not None:
        shape = shape[:-1] + (width,)
    return pl.BlockSpec((None,) + shape, lambda *_: (layer,) + (0,) * (len(shape) - 1) + (block,))


def _params(sem, vmem=VMEM_LIMIT):
    return pltpu.CompilerParams(dimension_semantics=sem, vmem_limit_bytes=vmem)


def _dot(a, b):
    return jnp.dot(a, b, preferred_element_type=F32)


def _dot_nt(a, b):
    return lax.dot_general(a, b, (((1,), (1,)), ((), ())), preferred_element_type=F32)


def _dot_tn(a, b):
    return lax.dot_general(a, b, (((0,), (0,)), ((), ())), preferred_element_type=F32)


def _rope_table_kernel(pos_ref, freq_ref, sgn_ref, base_ref, cos_ref, sin_ref):
    ang = pos_ref[0].astype(F32) * freq_ref[...]
    act = sgn_ref[...] != 0.0
    cos_ref[0] = jnp.where(act, jnp.cos(ang), base_ref[...])
    sin_ref[0] = jnp.sin(ang) * sgn_ref[...]


def _rope_tables(positions, freq, sgn, base, ts=1024):
    B, S = positions.shape
    pos3 = positions.reshape(B, S, 1)
    tab = jax.ShapeDtypeStruct((B, S, 128), F32)
    return pl.pallas_call(
        _rope_table_kernel,
        out_shape=(tab, tab),
        grid=(B, S // ts),
        in_specs=[pl.BlockSpec((1, ts, 1), lambda b, s: (b, s, 0)),
                  _const_spec((1, 128)), _const_spec((1, 128)), _const_spec((1, 128))],
        out_specs=(pl.BlockSpec((1, ts, 128), lambda b, s: (b, s, 0)),
                   pl.BlockSpec((1, ts, 128), lambda b, s: (b, s, 0))),
        compiler_params=_params(("parallel", "parallel")),
        name="rope_tables",
    )(pos3, freq, sgn, base)


def _rope_lanes(x, cos, sin, lo_mask, half):
    swapped = jnp.where(lo_mask, pltpu.roll(x, 128 - half, 1), pltpu.roll(x, half, 1))
    return x * cos + swapped * sin


def _mla_proj_kernel(x_ref, cos_ref, sin_ref, wlat_ref, qn_ref, kvn_ref, wuq_ref, wukv_ref,
                     q_ref, k_ref, v_ref):
    tm = x_ref.shape[1]
    blocks = [slice(r, r + PROJ_ROWS) for r in range(0, tm, PROJ_ROWS)]

    def rms(v, g):
        return v * lax.rsqrt(jnp.mean(jnp.square(v), -1, keepdims=True) + RMS_EPS) * g

    def rotate(t, c, s):
        return t * c + pltpu.roll(t, 128 - MLA_ROPE // 2, 1) * s

    lat = [_dot(x_ref[0, rs, :].astype(BF16), wlat_ref[...]) for rs in blocks]
    q = [_dot(rms(t[:, :MLA_Q_LORA], qn_ref[...]).astype(BF16), wuq_ref[...]) for t in lat]
    kv = [_dot(rms(t[:, MLA_Q_LORA:MLA_Q_LORA + MLA_KV_LORA], kvn_ref[...]).astype(BF16), wukv_ref[...])
          for t in lat]
    lane = lax.broadcasted_iota(jnp.int32, (PROJ_ROWS, HEAD_PAD), 1)
    ones_lane = jnp.where(lane == MLA_V, 1.0, 0.0)
    for r, rs in enumerate(blocks):
        cos = cos_ref[0, rs, :]
        sin = sin_ref[0, rs, :]
        cos_q = cos * (ATTN_SCALE * LOG2E)
        sin_q = sin * (ATTN_SCALE * LOG2E)
        k_rot = rotate(lat[r][:, MLA_Q_LORA + MLA_KV_LORA:], cos, sin)
        for h in range(HEADS):
            sl = slice(h * HEAD_PAD, (h + 1) * HEAD_PAD)
            vl = slice((HEADS + h) * HEAD_PAD, (HEADS + h + 1) * HEAD_PAD)
            q_ref[0, rs, sl] = rotate(q[r][:, sl], cos_q, sin_q).astype(BF16)
            k_ref[0, rs, sl] = (kv[r][:, sl] + k_rot).astype(BF16)
            v_ref[0, rs, sl] = (kv[r][:, vl] + ones_lane).astype(BF16)


def _mla_proj(x, cos, sin, layer, w_a, qn, kvn, wuq, wukv, tm=512):
    B, S, _ = x.shape
    tok = lambda c: pl.BlockSpec((1, tm, c), lambda b, s: (b, s, 0))
    out = jax.ShapeDtypeStruct((B, S, HEADS * HEAD_PAD), BF16)
    return pl.pallas_call(
        _mla_proj_kernel,
        out_shape=(out, out, out),
        grid=(B, S // tm),
        in_specs=[tok(D_MODEL), tok(128), tok(128),
                  _layer_spec(w_a, layer, LAT_COLS, (4 * WIDTH + 2 * WIDTH) // LAT_COLS),
                  _layer_spec(qn, layer), _layer_spec(kvn, layer), _layer_spec(wuq, layer), _layer_spec(wukv, layer)],
        out_specs=(tok(HEADS * HEAD_PAD), tok(HEADS * HEAD_PAD), tok(HEADS * HEAD_PAD)),
        compiler_params=_params(("parallel", "parallel")),
        name="mla_proj",
    )(x, cos, sin, w_a, qn, kvn, wuq, wukv)


def _attn_kernel(q_ref, k_ref, v_ref, o_ref, m_s, acc_s, s_a, s_b, *, tq, tk):
    i = pl.program_id(2)
    ratio = tq // tk
    assert ratio == 2
    m_s[...] = jnp.full(m_s.shape, NEG, F32)
    acc_s[...] = jnp.zeros(acc_s.shape, F32)
    heads = [slice(hh * HEAD_PAD, (hh + 1) * HEAD_PAD) for hh in range(2)]

    top, bottom = slice(0, tk), slice(tk, tq)

    def scores(j, hh, rows):
        start = pl.multiple_of(j * tk, tk)
        return _dot_nt(q_ref[0, rows, heads[hh]], k_ref[0, pl.ds(start, tk), heads[hh]])

    def accumulate(j, hh, s, rows):
        start = pl.multiple_of(j * tk, tk)
        m = m_s[hh, rows, :]
        m_new = jnp.maximum(m, jnp.max(s, -1, keepdims=True))
        p = jnp.concatenate([jnp.exp2(s[:, c * 128:(c + 1) * 128] - m_new) for c in range(tk // 128)], axis=1)
        pv = _dot(p.astype(BF16), v_ref[0, pl.ds(start, tk), heads[hh]])
        acc_s[hh, rows, :] = jnp.exp2(m - m_new) * acc_s[hh, rows, :] + pv
        m_s[hh, rows, :] = m_new

    for rows in (top, bottom):
        for hh in range(2):
            s_a[hh, rows, :] = scores(0, hh, rows)

    def body(n, _):
        for cur, nxt, j in ((s_a, s_b, 2 * n), (s_b, s_a, 2 * n + 1)):
            for rows in (top, bottom):
                for hh in range(2):
                    nxt[hh, rows, :] = scores(j + 1, hh, rows)
                    accumulate(j, hh, cur[hh, rows, :], rows)
        return 0

    lax.fori_loop(0, i, body, 0)
    causal = (lax.broadcasted_iota(jnp.int32, (tk, tk), 1) <= lax.broadcasted_iota(jnp.int32, (tk, tk), 0))
    for hh in range(2):
        s_b[hh, bottom, :] = scores(2 * i + 1, hh, bottom)
        accumulate(2 * i, hh, jnp.where(causal, s_a[hh, top, :], NEG), top)
        accumulate(2 * i, hh, s_a[hh, bottom, :], bottom)
        accumulate(2 * i + 1, hh, jnp.where(causal, s_b[hh, bottom, :], NEG), bottom)
    for hh in range(2):
        acc = acc_s[hh]
        o = acc[:, :MLA_V] / acc[:, MLA_V:MLA_V + 1]
        o_ref[0, :, hh * MLA_V:(hh + 1) * MLA_V] = o.astype(o_ref.dtype)


def _attention(q, k, v, tq=1024, tk=512):
    B, S, _ = q.shape
    return pl.pallas_call(
        functools.partial(_attn_kernel, tq=tq, tk=tk),
        out_shape=jax.ShapeDtypeStruct((B, S, WIDTH), BF16),
        grid=(B, HEADS // 2, S // tq),
        in_specs=[pl.BlockSpec((1, tq, 2 * HEAD_PAD), lambda b, h, i: (b, i, h)),
                  pl.BlockSpec((1, S, 2 * HEAD_PAD), lambda b, h, i: (b, 0, h)),
                  pl.BlockSpec((1, S, 2 * HEAD_PAD), lambda b, h, i: (b, 0, h))],
        out_specs=pl.BlockSpec((1, tq, 2 * MLA_V), lambda b, h, i: (b, i, h)),
        scratch_shapes=[pltpu.VMEM((2, tq, HEAD_PAD), F32), pltpu.VMEM((2, tq, HEAD_PAD), F32),
                        pltpu.VMEM((2, tq, tk), F32), pltpu.VMEM((2, tq, tk), F32)],
        compiler_params=_params(("parallel", "parallel", "arbitrary")),
        name="mla_attention",
    )(q, k, v)


def _lru_kernel(x_ref, w_ref, cw_ref, cb_ref, wg_ref, bg_ref, lam_ref, y_ref, ubuf, hcar, h_s, *, ts):
    @pl.when(pl.program_id(1) == 0)
    def _():
        ubuf[0:8, :] = jnp.zeros((8, WIDTH), F32)
        hcar[...] = jnp.zeros_like(hcar)

    nb = ts // LRU_ROWS
    blocks = [slice(r * LRU_ROWS, (r + 1) * LRU_ROWS) for r in range(nb)]
    z = [_dot(x_ref[0, rs, :].astype(BF16), w_ref[...]) for rs in blocks]
    xc = []
    for r, rs in enumerate(blocks):
        u = z[r][:, :WIDTH]
        lo = 8 + r * LRU_ROWS
        ubuf[lo:lo + LRU_ROWS, :] = u
        acc = cb_ref[...] + cw_ref[3:4, :] * u
        for j in range(1, LRU_CONV):
            acc = acc + cw_ref[3 - j:4 - j, :] * ubuf[lo - j:lo - j + LRU_ROWS, :]
        xc.append(acc)
    ubuf[0:8, :] = z[nb - 1][LRU_ROWS - 8:, :WIDTH]

    th = [jnp.tanh(_dot(xc[r].astype(BF16), wg_ref[...]) + bg_ref[...]) for r in range(nb)]
    lam = lam_ref[...]
    softplus_neg = jnp.maximum(-lam, 0.0) + jnp.log1p(jnp.exp(-jnp.abs(lam)))
    c1 = (-0.5 * LRU_C) * softplus_neg
    groups = LRU_ROWS // 8
    sub = lax.broadcasted_iota(jnp.int32, (groups, 8, WIDTH), 1)
    a3, b3 = [], []
    for r in range(nb):
        log_a = th[r][:, :WIDTH] * c1 + c1
        a = jnp.exp(log_a)
        t = jnp.tanh(log_a)
        b_sq = -t * (1.0 + a * a)
        root = jnp.where(b_sq > 0.0, b_sq * lax.rsqrt(b_sq), 0.0)
        b = root * ((th[r][:, WIDTH:] + 1.0) * (0.5 * xc[r]))
        a = a.reshape(groups, 8, WIDTH)
        b = b.reshape(groups, 8, WIDTH)
        for d in (1, 2, 4):
            keep = sub >= d
            b = a * jnp.where(keep, pltpu.roll(b, d, 1), 0.0) + b
            a = a * jnp.where(keep, pltpu.roll(a, d, 1), 1.0)
        a3.append(a)
        b3.append(b)
    carry = hcar[...]
    for r in range(nb):
        for n in range(groups):
            h = a3[r][n] * carry + b3[r][n]
            h_s[r * LRU_ROWS + n * 8:r * LRU_ROWS + (n + 1) * 8, :] = h
            carry = jnp.broadcast_to(h[7:8, :], (8, WIDTH))
    hcar[...] = carry
    for r, rs in enumerate(blocks):
        half_g = z[r][:, WIDTH:]
        y_ref[0, rs, :] = (h_s[rs, :] * (half_g * jnp.tanh(half_g) + half_g)).astype(y_ref.dtype)


def _lru(x, layer, w_a, cw, cb, wg, bg, lam, ts=512):
    B, S, _ = x.shape
    return pl.pallas_call(
        functools.partial(_lru_kernel, ts=ts),
        out_shape=jax.ShapeDtypeStruct((B, S, WIDTH), BF16),
        grid=(B, S // ts),
        in_specs=[pl.BlockSpec((1, ts, D_MODEL), lambda b, s: (b, s, 0)),
                  _layer_spec(w_a, layer, 2 * WIDTH, 2), _layer_spec(cw, layer), _layer_spec(cb, layer),
                  _layer_spec(wg, layer), _layer_spec(bg, layer), _layer_spec(lam, layer)],
        out_specs=pl.BlockSpec((1, ts, WIDTH), lambda b, s: (b, s, 0)),
        scratch_shapes=[pltpu.VMEM((ts + 8, WIDTH), F32), pltpu.VMEM((8, WIDTH), F32),
                        pltpu.VMEM((ts, WIDTH), F32)],
        compiler_params=_params(("parallel", "arbitrary")),
        name="rglru",
    )(x, w_a, cw, cb, wg, bg, lam)


def _ret_kernel(x_ref, cos_ref, sin_ref, w_ref, decay_ref, qw_ref, kw_ref, cd_ref, gn_ref, avg_ref, y_ref,
                state, *, ts):
    C = RET_CHUNK
    n_chunks = ts // C

    @pl.when(pl.program_id(1) == 0)
    def _():
        state[...] = jnp.zeros_like(state)

    xb = x_ref[0].astype(BF16)
    cos = cos_ref[0]
    sin = sin_ref[0]
    lo = (lax.broadcasted_iota(jnp.int32, cos.shape, 1) % RET_DK) < RET_DK // 2
    first = lax.broadcasted_iota(jnp.int32, (C, 128), 1) < RET_DK
    same_head = (lax.broadcasted_iota(jnp.int32, (128, 128), 0) < RET_DK) == (
        lax.broadcasted_iota(jnp.int32, (128, 128), 1) < RET_DK)

    def group_mean(val):
        hi = val.astype(BF16)
        lo_part = (val - hi.astype(F32)).astype(BF16)
        return _dot(jnp.concatenate([hi, lo_part], axis=1), avg_ref[...])

    pairs = range(HEADS // 2)
    chunks = range(n_chunks)
    tiles = [(p, c) for p in pairs for c in chunks]
    lanes = [slice(p * 128, (p + 1) * 128) for p in pairs]
    z = _dot(xb, w_ref[...])
    qb = [_rope_lanes(z[:, p * 512:p * 512 + 128], cos, sin, lo, RET_DK // 2) for p in pairs]
    kb = [_rope_lanes(z[:, p * 512 + 128:p * 512 + 256], cos, sin, lo, RET_DK // 2) for p in pairs]
    vb = [z[:, p * 512 + 256:p * 512 + 384].astype(BF16) for p in pairs]
    rows = [slice(c * C, (c + 1) * C) for c in chunks]
    q2 = {(p, c): jnp.concatenate([jnp.where(first, qb[p][rows[c]], 0.0), jnp.where(first, 0.0, qb[p][rows[c]])],
                                  axis=0).astype(BF16) for p, c in tiles}
    scores = {(p, c): _dot_nt(q2[p, c], kb[p][rows[c]].astype(BF16)) * decay_ref[p] for p, c in tiles}
    inner2 = {(p, c): _dot(scores[p, c].astype(BF16), vb[p][rows[c]]) for p, c in tiles}
    inner = {t: jnp.where(first, inner2[t][:C], inner2[t][C:]) for t in tiles}
    kv = {(p, c): jnp.where(same_head, _dot_tn((kb[p][rows[c]] * kw_ref[:, lanes[p]]).astype(BF16), vb[p][rows[c]]),
                            0.0) for p, c in tiles}
    q_weighted = {(p, c): (qb[p][rows[c]] * qw_ref[:, lanes[p]]).astype(BF16) for p, c in tiles}
    s_prev = [state[p] for p in pairs]
    outs = {}
    for c in chunks:
        for p in pairs:
            outs[p, c] = inner[p, c] + _dot(q_weighted[p, c], s_prev[p].astype(BF16))
            s_prev[p] = cd_ref[:, lanes[p]] * s_prev[p] + kv[p, c]
    for p in pairs:
        state[p] = s_prev[p]
    o = [jnp.concatenate([outs[p, c] for c in chunks], axis=0) for p in pairs]
    mean = [group_mean(o[p]) for p in pairs]
    dev = [o[p] - mean[p] for p in pairs]
    var = [group_mean(dev[p] * dev[p]) for p in pairs]
    for p in pairs:
        half_g = z[:, p * 512 + 384:(p + 1) * 512]
        y = dev[p] * lax.rsqrt(var[p] + LN_EPS) * gn_ref[:, lanes[p]] * (half_g * jnp.tanh(half_g) + half_g)
        y_ref[0, :, lanes[p]] = y.astype(y_ref.dtype)


def _retention(x, cos, sin, layer, w_a, decay, qw, kw, cd, gn, avg, ts=512):
    B, S, _ = x.shape
    tok = lambda c: pl.BlockSpec((1, ts, c), lambda b, s: (b, s, 0))
    return pl.pallas_call(
        functools.partial(_ret_kernel, ts=ts),
        out_shape=jax.ShapeDtypeStruct((B, S, WIDTH), BF16),
        grid=(B, S // ts),
        in_specs=[tok(D_MODEL), tok(128), tok(128), _layer_spec(w_a, layer, 4 * WIDTH, 0),
                  _const_spec(decay.shape), _const_spec(qw.shape), _const_spec(kw.shape), _const_spec(cd.shape),
                  _layer_spec(gn, layer), _const_spec(avg.shape)],
        out_specs=tok(WIDTH),
        scratch_shapes=[pltpu.VMEM((HEADS // 2, 128, 128), F32)],
        compiler_params=_params(("parallel", "arbitrary")),
        name="retention",
    )(x, cos, sin, w_a, decay, qw, kw, cd, gn, avg)


def _merge_kernel(x_ref, o_ref, yl_ref, yr_ref, wg_ref, bm_ref, wb_ref, wo_ref, lg_ref, lb_ref, out_ref):
    tm = x_ref.shape[1]
    blocks = [slice(r, r + MERGE_ROWS) for r in range(0, tm, MERGE_ROWS)]
    zg = [_dot(x_ref[0, rs, :].astype(BF16), wg_ref[...]) for rs in blocks]
    mixed = []
    for r, rs in enumerate(blocks):
        half_g = zg[r][:, :WIDTH]
        y_mla = (o_ref[0, rs, :].astype(F32) * (half_g * jnp.tanh(half_g) + half_g)).astype(BF16)
        ys = (y_mla, yl_ref[0, rs, :], yr_ref[0, rs, :])
        total = None
        for n in range(N_BRANCHES):
            lo = WIDTH + n * D_MODEL
            gate = 0.5 * jnp.tanh(zg[r][:, lo:lo + D_MODEL] + bm_ref[:, n * D_MODEL:(n + 1) * D_MODEL]) + 0.5
            term = gate * _dot(ys[n], wb_ref[n])
            total = term if total is None else total + term
        mixed.append(total.astype(BF16))
    proj = [_dot(mixed[r], wo_ref[...]) for r in range(len(blocks))]
    for r, rs in enumerate(blocks):
        xf = ALPHA * x_ref[0, rs, :] + proj[r]
        mu = jnp.mean(xf, -1, keepdims=True)
        var = jnp.mean(jnp.square(xf - mu), -1, keepdims=True)
        out_ref[0, rs, :] = (xf - mu) * lax.rsqrt(var + LN_EPS) * lg_ref[...] + lb_ref[...]


def _merge(x, o, yl, yr, layer, wg, bm, wb, wo, lg, lb, tm=512):
    B, S, _ = x.shape
    tok = lambda c: pl.BlockSpec((1, tm, c), lambda b, s: (b, s, 0))
    return pl.pallas_call(
        _merge_kernel,
        out_shape=jax.ShapeDtypeStruct((B, S, D_MODEL), F32),
        grid=(B, S // tm),
        in_specs=[tok(D_MODEL), tok(WIDTH), tok(WIDTH), tok(WIDTH),
                  _layer_spec(wg, layer), _layer_spec(bm, layer), _layer_spec(wb, layer),
                  _layer_spec(wo, layer), _layer_spec(lg, layer), _layer_spec(lb, layer)],
        out_specs=tok(D_MODEL),
        compiler_params=_params(("parallel", "parallel")),
        name="merge_out_ln",
    )(x, o, yl, yr, wg, bm, wb, wo, lg, lb)


def _rope_patterns():
    lane = jnp.arange(128)
    half = MLA_ROPE // 2
    inv = ROPE_BASE ** (-jnp.arange(half, dtype=F32) / half)
    in_rope = (lane >= MLA_NOPE) & (lane < MLA_NOPE + MLA_ROPE)
    idx = jnp.clip(lane - MLA_NOPE, 0, MLA_ROPE - 1) % half
    mla_freq = jnp.where(in_rope, inv[idx], 0.0).astype(F32)[None, :]
    mla_sgn = jnp.where(in_rope, jnp.where(lane < MLA_NOPE + half, -1.0, 1.0), 0.0).astype(F32)[None, :]
    mla_base = jnp.where(lane < MLA_NOPE, 1.0, 0.0).astype(F32)[None, :]
    half = RET_DK // 2
    inv = ROPE_BASE ** (-jnp.arange(half, dtype=F32) / half)
    ret_freq = inv[lane % half].astype(F32)[None, :]
    ret_sgn = jnp.where((lane % RET_DK) < half, -1.0, 1.0).astype(F32)[None, :]
    return (mla_freq, mla_sgn, mla_base), (ret_freq, ret_sgn, jnp.ones((1, 128), F32))


def _retention_constants():
    C = RET_CHUNK
    log_gamma = jnp.log1p(-jnp.exp2(-5.0 - jnp.arange(HEADS, dtype=F32)))
    idx = jnp.arange(C, dtype=F32)
    diff = idx[:, None] - idx[None, :]
    decay = jnp.where(diff[None] >= 0, jnp.exp(diff[None] * log_gamma[:, None, None]), 0.0)
    k_w = jnp.exp((C - 1.0 - idx)[:, None] * log_gamma[None, :])
    q_w = jnp.exp((idx + 1.0)[:, None] * log_gamma[None, :])
    chunk_decay = jnp.exp(C * log_gamma)
    expand = lambda t: jnp.repeat(t, RET_DK, axis=-1)
    lane_head = jnp.arange(128) // RET_DK
    avg = jnp.where(lane_head[:, None] == lane_head[None, :], 1.0 / RET_DK, 0.0)
    avg = jnp.concatenate([avg, avg], axis=0).astype(BF16)
    return (decay.reshape(HEADS // 2, 2 * C, C), expand(q_w), expand(k_w), expand(chunk_decay[None, :]), avg)


def _block_diag(w):
    d, n, a, b = w.shape
    return (jnp.eye(n, dtype=w.dtype)[None, :, None, :, None] * w[:, :, :, None, :]).reshape(d, n * a, n * b)


def _prepare_weights(w_in, mla_w_uq, mla_w_ukv, lru_w_r, lru_w_i):
    offs = [0]
    for s in IN_SIZES:
        offs.append(offs[-1] + s)
    col = lambda i: w_in[:, :, offs[i]:offs[i + 1]]
    zeros = lambda n: jnp.zeros((DEPTH, D_MODEL, n), F32)
    half = MLA_ROPE // 2
    pad = HEAD_PAD - MLA_NOPE - MLA_ROPE - half
    w_lat = jnp.concatenate([col(0), col(1), zeros(MLA_NOPE), col(2), col(2)[:, :, :half], zeros(pad)], axis=2)
    w_lru = jnp.concatenate([col(4), 0.5 * col(5)], axis=2)
    w_ret = jnp.stack([col(6), RET_DK ** -0.5 * col(7), col(8), 0.5 * col(9)], axis=2)
    w_ret = w_ret.reshape(DEPTH, D_MODEL, 4, HEADS // 2, 128).transpose(0, 1, 3, 2, 4)
    w_ret = w_ret.reshape(DEPTH, D_MODEL, 4 * WIDTH)
    w_a = jnp.concatenate([w_ret, w_lru, w_lat], axis=2).astype(BF16)
    w_gm = (0.5 * jnp.concatenate([col(3), col(10)], axis=2)).astype(BF16)

    w_uq = mla_w_uq.reshape(DEPTH, MLA_Q_LORA, HEADS, MLA_NOPE + MLA_ROPE)
    w_uq = jnp.concatenate([w_uq, w_uq[..., MLA_NOPE:MLA_NOPE + half],
                            jnp.zeros((DEPTH, MLA_Q_LORA, HEADS, pad), F32)], axis=3)
    w_uq = w_uq.reshape(DEPTH, MLA_Q_LORA, HEADS * HEAD_PAD).astype(BF16)
    w_ukv = mla_w_ukv.reshape(DEPTH, MLA_KV_LORA, HEADS, MLA_NOPE + MLA_V)
    tile_pad = ((0, 0), (0, 0), (0, 0), (0, HEAD_PAD - MLA_NOPE))
    w_ukv = jnp.concatenate([jnp.pad(w_ukv[..., :MLA_NOPE], tile_pad).reshape(DEPTH, MLA_KV_LORA, HEADS * HEAD_PAD),
                             jnp.pad(w_ukv[..., MLA_NOPE:], tile_pad).reshape(DEPTH, MLA_KV_LORA, HEADS * HEAD_PAD)],
                            axis=2).astype(BF16)
    w_gates = (0.5 * jnp.concatenate([_block_diag(lru_w_r), _block_diag(lru_w_i)], axis=2)).astype(BF16)
    return w_a, w_gm, w_uq, w_ukv, w_gates


def kernel(x, positions, w_in, b_merge, mla_q_norm, mla_w_uq, mla_kv_norm, mla_w_ukv, lru_conv_w, lru_conv_b,
           lru_w_r, lru_b_r, lru_w_i, lru_b_i, lru_lambda, ret_gn_g, w_branch, w_out, ln_g, ln_b):
    mla_pattern, ret_pattern = _rope_patterns()
    mla_cos, mla_sin = _rope_tables(positions, *mla_pattern)
    ret_cos, ret_sin = _rope_tables(positions, *ret_pattern)
    decay, q_w, k_w, chunk_decay, avg = _retention_constants()

    w_a, w_gm, w_uq, w_ukv, w_gates = _prepare_weights(w_in, mla_w_uq, mla_w_ukv, lru_w_r, lru_w_i)
    w_b = w_branch.astype(BF16)
    w_o = w_out.astype(BF16)
    row = lambda v: v[:, None, :]
    b_gates = row(0.5 * jnp.concatenate([lru_b_r, lru_b_i], axis=1))
    b_m = row(0.5 * b_merge)

    for l in range(DEPTH):
        q, k, v = _mla_proj(x, mla_cos, mla_sin, l, w_a, row(mla_q_norm), row(mla_kv_norm), w_uq, w_ukv)
        o_mla = _attention(q, k, v)
        y_lru = _lru(x, l, w_a, lru_conv_w, row(lru_conv_b), w_gates, b_gates, row(lru_lambda))
        y_ret = _retention(x, ret_cos, ret_sin, l, w_a, decay, q_w, k_w, chunk_decay, row(ret_gn_g), avg)
        x = _merge(x, o_mla, y_lru, y_ret, l, w_gm, b_m, w_b, w_o, row(ln_g), row(ln_b))
    return x
```

```python
import functools

import jax
import jax.numpy as jnp
import numpy as np
from jax import lax
from jax.experimental import pallas as pl
from jax.experimental.pallas import tpu as pltpu

F32 = jnp.float32
BF16 = jnp.bfloat16

D_MODEL = 1024
DEPTH = 2
HEADS = 8
MLA_Q_LORA = 384
MLA_KV_LORA = 256
MLA_NOPE = 64
MLA_ROPE = 32
MLA_V = 64
HEAD_PAD = 128
WIDTH = 512
LRU_CONV = 4
LRU_C = 8.0
LRU_ROWS = 256
MERGE_ROWS = 256
PROJ_ROWS = 256
RET_DK = 64
RET_CHUNK = 128
N_BRANCHES = 3
ROPE_BASE = 10000.0
LN_EPS = 1e-5
RMS_EPS = 1e-6
ALPHA = (2 * DEPTH) ** 0.25
ATTN_SCALE = (MLA_NOPE + MLA_ROPE) ** -0.5
NEG = -1e30
LOG2E = 1.4426950408889634

IN_SIZES = (MLA_Q_LORA, MLA_KV_LORA, MLA_ROPE, WIDTH, WIDTH, WIDTH, WIDTH, WIDTH, WIDTH, WIDTH,
            N_BRANCHES * D_MODEL)
LAT_COLS = MLA_Q_LORA + MLA_KV_LORA + HEAD_PAD

VMEM_LIMIT = 56 * 1024 * 1024


def _const_spec(shape):
    return pl.BlockSpec(shape, lambda *_: (0,) * len(shape))


def _layer_spec(arr, layer, width=None, block=0):
    shape = arr.shape[1:]
    if width is not None:
        shape = shape[:-1] + (width,)
    return pl.BlockSpec((None,) + shape, lambda *_: (layer,) + (0,) * (len(shape) - 1) + (block,))


def _params(sem, vmem=VMEM_LIMIT):
    return pltpu.CompilerParams(dimension_semantics=sem, vmem_limit_bytes=vmem)


def _dot(a, b):
    return jnp.dot(a, b, preferred_element_type=F32)


def _dot_nt(a, b):
    return lax.dot_general(a, b, (((1,), (1,)), ((), ())), preferred_element_type=F32)


def _dot_tn(a, b):
    return lax.dot_general(a, b, (((0,), (0,)), ((), ())), preferred_element_type=F32)


def _rope_table_kernel(pos_ref, freq_ref, msgn_ref, mbase_ref, rsgn_ref, mcos_ref, msin_ref, rcos_ref, rsin_ref):
    ang = pos_ref[0] * freq_ref[...]
    lane = lax.broadcasted_iota(jnp.int32, ang.shape, 1)
    first_half = (lane >= MLA_NOPE) & (lane < MLA_NOPE + MLA_ROPE // 2)
    second_half = (lane >= MLA_NOPE + MLA_ROPE // 2) & (lane < MLA_NOPE + MLA_ROPE)
    for trig, sgn_m, sgn_r, base, m_ref, r_ref in ((jnp.cos(ang), None, None, mbase_ref[...], mcos_ref, rcos_ref),
                                                   (jnp.sin(ang), msgn_ref[...], rsgn_ref[...], 0.0, msin_ref, rsin_ref)):
        mla = jnp.where(first_half, pltpu.roll(trig, 32, 1), jnp.where(second_half, pltpu.roll(trig, 48, 1), base))
        ret = jnp.where(lane < 32, trig, pltpu.roll(trig, 32, 1))
        ret = jnp.where(lane < 64, ret, pltpu.roll(ret, 64, 1))
        m_ref[0] = mla if sgn_m is None else mla * sgn_m
        r_ref[0] = ret if sgn_r is None else ret * sgn_r


def _rope_tables(positions, freq, mla_sgn, mla_base, ret_sgn, ts=1024):
    B, S, _ = positions.shape
    tab = jax.ShapeDtypeStruct((B, S, 128), F32)
    tok = pl.BlockSpec((1, ts, 128), lambda b, s: (b, s, 0))
    return pl.pallas_call(
        _rope_table_kernel,
        out_shape=(tab, tab, tab, tab),
        grid=(B, S // ts),
        in_specs=[tok] + [_const_spec((1, 128))] * 4,
        out_specs=(tok, tok, tok, tok),
        compiler_params=_params(("parallel", "parallel")),
        name="rope_tables",
    )(positions, freq, mla_sgn, mla_base, ret_sgn)


def _rope_lanes(x, cos, sin, lo_mask, half):
    swapped = jnp.where(lo_mask, pltpu.roll(x, 128 - half, 1), pltpu.roll(x, half, 1))
    return x * cos + swapped * sin


def _mla_proj_kernel(x_ref, cos_ref, sin_ref, wlat_ref, qn_ref, kvn_ref, wuq_ref, wukv_ref,
                     q_ref, k_ref, v_ref):
    tm = x_ref.shape[1]
    blocks = [slice(r, r + PROJ_ROWS) for r in range(0, tm, PROJ_ROWS)]

    def rms(v, g):
        return v * lax.rsqrt(jnp.mean(jnp.square(v), -1, keepdims=True) + RMS_EPS) * g

    def rotate(t, c, s):
        return t * c + pltpu.roll(t, 128 - MLA_ROPE // 2, 1) * s

    lat = [_dot(x_ref[0, rs, :].astype(BF16), wlat_ref[...]) for rs in blocks]
    q = [_dot(rms(t[:, :MLA_Q_LORA], qn_ref[...]).astype(BF16), wuq_ref[...]) for t in lat]
    kv = [_dot(rms(t[:, MLA_Q_LORA:MLA_Q_LORA + MLA_KV_LORA], kvn_ref[...]).astype(BF16), wukv_ref[...])
          for t in lat]
    lane = lax.broadcasted_iota(jnp.int32, (PROJ_ROWS, HEAD_PAD), 1)
    ones_lane = jnp.where(lane == MLA_V, 1.0, 0.0)
    for r, rs in enumerate(blocks):
        cos = cos_ref[0, rs, :]
        sin = sin_ref[0, rs, :]
        cos_q = cos * (ATTN_SCALE * LOG2E)
        sin_q = sin * (ATTN_SCALE * LOG2E)
        k_rot = rotate(lat[r][:, MLA_Q_LORA + MLA_KV_LORA:], cos, sin)
        for h in range(HEADS):
            sl = slice(h * HEAD_PAD, (h + 1) * HEAD_PAD)
            vl = slice((HEADS + h) * HEAD_PAD, (HEADS + h + 1) * HEAD_PAD)
            q_ref[0, rs, sl] = rotate(q[r][:, sl], cos_q, sin_q).astype(BF16)
            k_ref[0, rs, sl] = (kv[r][:, sl] + k_rot).astype(BF16)
            v_ref[0, rs, sl] = (kv[r][:, vl] + ones_lane).astype(BF16)


def _mla_proj(x, cos, sin, layer, wlat, qn, kvn, wuq, wukv, tm=512):
    B, S, _ = x.shape
    tok = lambda c: pl.BlockSpec((1, tm, c), lambda b, s: (b, s, 0))
    out = jax.ShapeDtypeStruct((B, S, HEADS * HEAD_PAD), BF16)
    return pl.pallas_call(
        _mla_proj_kernel,
        out_shape=(out, out, out),
        grid=(B, S // tm),
        in_specs=[tok(D_MODEL), tok(128), tok(128),
                  _layer_spec(wlat, layer),
                  _layer_spec(qn, layer), _layer_spec(kvn, layer), _layer_spec(wuq, layer), _layer_spec(wukv, layer)],
        out_specs=(tok(HEADS * HEAD_PAD), tok(HEADS * HEAD_PAD), tok(HEADS * HEAD_PAD)),
        compiler_params=_params(("parallel", "parallel")),
        name="mla_proj",
    )(x, cos, sin, wlat, qn, kvn, wuq, wukv)


def _attn_kernel(q_ref, k_ref, v_ref, o_ref, m_s, acc_s, s_a, s_b, *, tq, tk):
    i = pl.program_id(2)
    ratio = tq // tk
    assert ratio == 2
    m_s[...] = jnp.full(m_s.shape, NEG, F32)
    acc_s[...] = jnp.zeros(acc_s.shape, F32)
    heads = [slice(hh * HEAD_PAD, (hh + 1) * HEAD_PAD) for hh in range(2)]

    top, bottom = slice(0, tk), slice(tk, tq)

    def scores(j, hh, rows):
        start = pl.multiple_of(j * tk, tk)
        return _dot_nt(q_ref[0, rows, heads[hh]], k_ref[0, pl.ds(start, tk), heads[hh]])

    def accumulate(j, hh, s, rows):
        start = pl.multiple_of(j * tk, tk)
        m = m_s[hh, rows, :]
        m_new = jnp.maximum(m, jnp.max(s, -1, keepdims=True))
        p = jnp.concatenate([jnp.exp2(s[:, c * 128:(c + 1) * 128] - m_new) for c in range(tk // 128)], axis=1)
        pv = _dot(p.astype(BF16), v_ref[0, pl.ds(start, tk), heads[hh]])
        acc_s[hh, rows, :] = jnp.exp2(m - m_new) * acc_s[hh, rows, :] + pv
        m_s[hh, rows, :] = m_new

    for rows in (top, bottom):
        for hh in range(2):
            s_a[hh, rows, :] = scores(0, hh, rows)

    def body(n, _):
        for cur, nxt, j in ((s_a, s_b, 2 * n), (s_b, s_a, 2 * n + 1)):
            for rows in (top, bottom):
                for hh in range(2):
                    nxt[hh, rows, :] = scores(j + 1, hh, rows)
                    accumulate(j, hh, cur[hh, rows, :], rows)
        return 0

    lax.fori_loop(0, i, body, 0)
    causal = (lax.broadcasted_iota(jnp.int32, (tk, tk), 1) <= lax.broadcasted_iota(jnp.int32, (tk, tk), 0))
    for hh in range(2):
        s_b[hh, bottom, :] = scores(2 * i + 1, hh, bottom)
        accumulate(2 * i, hh, jnp.where(causal, s_a[hh, top, :], NEG), top)
        accumulate(2 * i, hh, s_a[hh, bottom, :], bottom)
        accumulate(2 * i + 1, hh, jnp.where(causal, s_b[hh, bottom, :], NEG), bottom)
    for hh in range(2):
        acc = acc_s[hh]
        o = acc[:, :MLA_V] / acc[:, MLA_V:MLA_V + 1]
        o_ref[0, :, hh * MLA_V:(hh + 1) * MLA_V] = o.astype(o_ref.dtype)


def _attention(q, k, v, tq=1024, tk=512):
    B, S, _ = q.shape
    return pl.pallas_call(
        functools.partial(_attn_kernel, tq=tq, tk=tk),
        out_shape=jax.ShapeDtypeStruct((B, S, WIDTH), BF16),
        grid=(B, HEADS // 2, S // tq),
        in_specs=[pl.BlockSpec((1, tq, 2 * HEAD_PAD), lambda b, h, i: (b, i, h)),
                  pl.BlockSpec((1, S, 2 * HEAD_PAD), lambda b, h, i: (b, 0, h)),
                  pl.BlockSpec((1, S, 2 * HEAD_PAD), lambda b, h, i: (b, 0, h))],
        out_specs=pl.BlockSpec((1, tq, 2 * MLA_V), lambda b, h, i: (b, i, h)),
        scratch_shapes=[pltpu.VMEM((2, tq, HEAD_PAD), F32), pltpu.VMEM((2, tq, HEAD_PAD), F32),
                        pltpu.VMEM((2, tq, tk), F32), pltpu.VMEM((2, tq, tk), F32)],
        compiler_params=_params(("parallel", "parallel", "arbitrary")),
        name="mla_attention",
    )(q, k, v)


def _lru_kernel(x_ref, w_ref, cw_ref, cb_ref, wg_ref, bg_ref, lam_ref, y_ref, ubuf, hcar, h_s, *, ts):
    @pl.when(pl.program_id(1) == 0)
    def _():
        ubuf[0:8, :] = jnp.zeros((8, WIDTH), F32)
        hcar[...] = jnp.zeros_like(hcar)

    nb = ts // LRU_ROWS
    blocks = [slice(r * LRU_ROWS, (r + 1) * LRU_ROWS) for r in range(nb)]
    z = [_dot(x_ref[0, rs, :].astype(BF16), w_ref[...]) for rs in blocks]
    xc = []
    for r, rs in enumerate(blocks):
        u = z[r][:, :WIDTH]
        lo = 8 + r * LRU_ROWS
        ubuf[lo:lo + LRU_ROWS, :] = u
        acc = cb_ref[...] + cw_ref[3:4, :] * u
        for j in range(1, LRU_CONV):
            acc = acc + cw_ref[3 - j:4 - j, :] * ubuf[lo - j:lo - j + LRU_ROWS, :]
        xc.append(acc)
    ubuf[0:8, :] = z[nb - 1][LRU_ROWS - 8:, :WIDTH]

    th = [jnp.tanh(_dot(xc[r].astype(BF16), wg_ref[...]) + bg_ref[...]) for r in range(nb)]
    lam = lam_ref[...]
    softplus_neg = jnp.maximum(-lam, 0.0) + jnp.log1p(jnp.exp(-jnp.abs(lam)))
    c1 = (-0.5 * LRU_C) * softplus_neg
    groups = LRU_ROWS // 8
    sub = lax.broadcasted_iota(jnp.int32, (groups, 8, WIDTH), 1)
    a3, b3 = [], []
    for r in range(nb):
        log_a = th[r][:, :WIDTH] * c1 + c1
        a = jnp.exp(log_a)
        t = jnp.tanh(log_a)
        b_sq = -t * (1.0 + a * a)
        root = jnp.where(b_sq > 0.0, b_sq * lax.rsqrt(b_sq), 0.0)
        b = root * ((th[r][:, WIDTH:] + 1.0) * (0.5 * xc[r]))
        a = a.reshape(groups, 8, WIDTH)
        b = b.reshape(groups, 8, WIDTH)
        for d in (1, 2, 4):
            keep = sub >= d
            b = a * jnp.where(keep, pltpu.roll(b, d, 1), 0.0) + b
            a = a * jnp.where(keep, pltpu.roll(a, d, 1), 1.0)
        a3.append(a)
        b3.append(b)
    carry = hcar[...]
    for r in range(nb):
        for n in range(groups):
            h = a3[r][n] * carry + b3[r][n]
            h_s[r * LRU_ROWS + n * 8:r * LRU_ROWS + (n + 1) * 8, :] = h
            carry = jnp.broadcast_to(h[7:8, :], (8, WIDTH))
    hcar[...] = carry
    for r, rs in enumerate(blocks):
        half_g = z[r][:, WIDTH:]
        y_ref[0, rs, :] = (h_s[rs, :] * (half_g * jnp.tanh(half_g) + half_g)).astype(y_ref.dtype)


def _lru(x, layer, w_a, cw, cb, wg, bg, lam, ts=512):
    B, S, _ = x.shape
    return pl.pallas_call(
        functools.partial(_lru_kernel, ts=ts),
        out_shape=jax.ShapeDtypeStruct((B, S, WIDTH), BF16),
        grid=(B, S // ts),
        in_specs=[pl.BlockSpec((1, ts, D_MODEL), lambda b, s: (b, s, 0)),
                  _layer_spec(w_a, layer, 2 * WIDTH, 2), _layer_spec(cw, layer), _layer_spec(cb, layer),
                  _layer_spec(wg, layer), _layer_spec(bg, layer), _layer_spec(lam, layer)],
        out_specs=pl.BlockSpec((1, ts, WIDTH), lambda b, s: (b, s, 0)),
        scratch_shapes=[pltpu.VMEM((ts + 8, WIDTH), F32), pltpu.VMEM((8, WIDTH), F32),
                        pltpu.VMEM((ts, WIDTH), F32)],
        compiler_params=_params(("parallel", "arbitrary")),
        name="rglru",
    )(x, w_a, cw, cb, wg, bg, lam)


def _ret_kernel(x_ref, cos_ref, sin_ref, w_ref, decay_ref, qw_ref, kw_ref, cd_ref, gn_ref, avg_ref, y_ref,
                state, *, ts):
    C = RET_CHUNK
    n_chunks = ts // C

    @pl.when(pl.program_id(1) == 0)
    def _():
        state[...] = jnp.zeros_like(state)

    xb = x_ref[0].astype(BF16)
    cos = cos_ref[0]
    sin = sin_ref[0]
    lo = (lax.broadcasted_iota(jnp.int32, cos.shape, 1) % RET_DK) < RET_DK // 2
    first = lax.broadcasted_iota(jnp.int32, (C, 128), 1) < RET_DK
    same_head = (lax.broadcasted_iota(jnp.int32, (128, 128), 0) < RET_DK) == (
        lax.broadcasted_iota(jnp.int32, (128, 128), 1) < RET_DK)

    def group_mean(val):
        hi = val.astype(BF16)
        lo_part = (val - hi.astype(F32)).astype(BF16)
        return _dot(jnp.concatenate([hi, lo_part], axis=1), avg_ref[...])

    pairs = range(HEADS // 2)
    chunks = range(n_chunks)
    tiles = [(p, c) for p in pairs for c in chunks]
    lanes = [slice(p * 128, (p + 1) * 128) for p in pairs]
    z = _dot(xb, w_ref[...])
    qb = [_rope_lanes(z[:, p * 512:p * 512 + 128], cos, sin, lo, RET_DK // 2) for p in pairs]
    kb = [_rope_lanes(z[:, p * 512 + 128:p * 512 + 256], cos, sin, lo, RET_DK // 2) for p in pairs]
    vb = [z[:, p * 512 + 256:p * 512 + 384].astype(BF16) for p in pairs]
    rows = [slice(c * C, (c + 1) * C) for c in chunks]
    q2 = {(p, c): jnp.concatenate([jnp.where(first, qb[p][rows[c]], 0.0), jnp.where(first, 0.0, qb[p][rows[c]])],
                                  axis=0).astype(BF16) for p, c in tiles}
    scores = {(p, c): _dot_nt(q2[p, c], kb[p][rows[c]].astype(BF16)) * decay_ref[p] for p, c in tiles}
    inner2 = {(p, c): _dot(scores[p, c].astype(BF16), vb[p][rows[c]]) for p, c in tiles}
    inner = {t: jnp.where(first, inner2[t][:C], inner2[t][C:]) for t in tiles}
    kv = {(p, c): jnp.where(same_head, _dot_tn((kb[p][rows[c]] * kw_ref[:, lanes[p]]).astype(BF16), vb[p][rows[c]]),
                            0.0) for p, c in tiles}
    q_weighted = {(p, c): (qb[p][rows[c]] * qw_ref[:, lanes[p]]).astype(BF16) for p, c in tiles}
    s_prev = [state[p] for p in pairs]
    outs = {}
    for c in chunks:
        for p in pairs:
            outs[p, c] = inner[p, c] + _dot(q_weighted[p, c], s_prev[p].astype(BF16))
            s_prev[p] = cd_ref[:, lanes[p]] * s_prev[p] + kv[p, c]
    for p in pairs:
        state[p] = s_prev[p]
    o = [jnp.concatenate([outs[p, c] for c in chunks], axis=0) for p in pairs]
    mean = [group_mean(o[p]) for p in pairs]
    dev = [o[p] - mean[p] for p in pairs]
    var = [group_mean(dev[p] * dev[p]) for p in pairs]
    for p in pairs:
        half_g = z[:, p * 512 + 384:(p + 1) * 512]
        y = dev[p] * lax.rsqrt(var[p] + LN_EPS) * gn_ref[:, lanes[p]] * (half_g * jnp.tanh(half_g) + half_g)
        y_ref[0, :, lanes[p]] = y.astype(y_ref.dtype)


def _retention(x, cos, sin, layer, w_a, decay, qw, kw, cd, gn, avg, ts=512):
    B, S, _ = x.shape
    tok = lambda c: pl.BlockSpec((1, ts, c), lambda b, s: (b, s, 0))
    return pl.pallas_call(
        functools.partial(_ret_kernel, ts=ts),
        out_shape=jax.ShapeDtypeStruct((B, S, WIDTH), BF16),
        grid=(B, S // ts),
        in_specs=[tok(D_MODEL), tok(128), tok(128), _layer_spec(w_a, layer, 4 * WIDTH, 0),
                  _const_spec(decay.shape), _const_spec(qw.shape), _const_spec(kw.shape), _const_spec(cd.shape),
                  _layer_spec(gn, layer), _const_spec(avg.shape)],
        out_specs=tok(WIDTH),
        scratch_shapes=[pltpu.VMEM((HEADS // 2, 128, 128), F32)],
        compiler_params=_params(("parallel", "arbitrary")),
        name="retention",
    )(x, cos, sin, w_a, decay, qw, kw, cd, gn, avg)


def _merge_kernel(x_ref, o_ref, yl_ref, yr_ref, wg_ref, bm_ref, wb_ref, wo_ref, lg_ref, lb_ref, out_ref):
    tm = x_ref.shape[1]
    blocks = [slice(r, r + MERGE_ROWS) for r in range(0, tm, MERGE_ROWS)]
    zg = [_dot(x_ref[0, rs, :].astype(BF16), wg_ref[...]) for rs in blocks]
    mixed = []
    for r, rs in enumerate(blocks):
        half_g = zg[r][:, :WIDTH]
        y_mla = (o_ref[0, rs, :].astype(F32) * (half_g * jnp.tanh(half_g) + half_g)).astype(BF16)
        ys = (y_mla, yl_ref[0, rs, :], yr_ref[0, rs, :])
        total = None
        for n in range(N_BRANCHES):
            lo = WIDTH + n * D_MODEL
            gate = 0.5 * jnp.tanh(zg[r][:, lo:lo + D_MODEL] + bm_ref[:, n * D_MODEL:(n + 1) * D_MODEL]) + 0.5
            term = gate * _dot(ys[n], wb_ref[n])
            total = term if total is None else total + term
        mixed.append(total.astype(BF16))
    proj = [_dot(mixed[r], wo_ref[...]) for r in range(len(blocks))]
    for r, rs in enumerate(blocks):
        xf = ALPHA * x_ref[0, rs, :] + proj[r]
        mu = jnp.mean(xf, -1, keepdims=True)
        var = jnp.mean(jnp.square(xf - mu), -1, keepdims=True)
        out_ref[0, rs, :] = (xf - mu) * lax.rsqrt(var + LN_EPS) * lg_ref[...] + lb_ref[...]


def _merge(x, o, yl, yr, layer, wg, bm, wb, wo, lg, lb, tm=512):
    B, S, _ = x.shape
    tok = lambda c: pl.BlockSpec((1, tm, c), lambda b, s: (b, s, 0))
    return pl.pallas_call(
        _merge_kernel,
        out_shape=jax.ShapeDtypeStruct((B, S, D_MODEL), F32),
        grid=(B, S // tm),
        in_specs=[tok(D_MODEL), tok(WIDTH), tok(WIDTH), tok(WIDTH),
                  _layer_spec(wg, layer), _layer_spec(bm, layer), _layer_spec(wb, layer),
                  _layer_spec(wo, layer), _layer_spec(lg, layer), _layer_spec(lb, layer)],
        out_specs=tok(D_MODEL),
        compiler_params=_params(("parallel", "parallel")),
        name="merge_out_ln",
    )(x, o, yl, yr, wg, bm, wb, wo, lg, lb)


def _rope_patterns():
    lane = np.arange(128)
    f32 = lambda a: np.asarray(a, np.float32)[None, :]
    half = MLA_ROPE // 2
    mla_inv = ROPE_BASE ** (-np.arange(half, dtype=np.float64) / half)
    in_rope = (lane >= MLA_NOPE) & (lane < MLA_NOPE + MLA_ROPE)
    mla_sgn = f32(np.where(in_rope, np.where(lane < MLA_NOPE + half, -1.0, 1.0), 0.0))
    mla_base = f32(np.where(lane < MLA_NOPE, 1.0, 0.0))
    half = RET_DK // 2
    ret_inv = ROPE_BASE ** (-np.arange(half, dtype=np.float64) / half)
    ret_sgn = f32(np.where((lane % RET_DK) < half, -1.0, 1.0))
    freq = f32(np.concatenate([ret_inv, mla_inv, np.zeros(128 - ret_inv.size - mla_inv.size)]))
    return freq, mla_sgn, mla_base, ret_sgn


def _retention_constants():
    C = RET_CHUNK
    log_gamma = np.log1p(-np.exp2(-5.0 - np.arange(HEADS, dtype=np.float64)))
    idx = np.arange(C, dtype=np.float64)
    diff = idx[:, None] - idx[None, :]
    decay = np.where(diff[None] >= 0, np.exp(diff[None] * log_gamma[:, None, None]), 0.0)
    k_w = np.exp((C - 1.0 - idx)[:, None] * log_gamma[None, :])
    q_w = np.exp((idx + 1.0)[:, None] * log_gamma[None, :])
    chunk_decay = np.exp(C * log_gamma)
    expand = lambda t: np.repeat(t, RET_DK, axis=-1).astype(np.float32)
    lane_head = np.arange(128) // RET_DK
    avg = np.where(lane_head[:, None] == lane_head[None, :], 1.0 / RET_DK, 0.0)
    avg = jnp.asarray(np.concatenate([avg, avg], axis=0), BF16)
    return (decay.reshape(HEADS // 2, 2 * C, C).astype(np.float32), expand(q_w), expand(k_w),
            expand(chunk_decay[None, :]), avg)


def _block_diag(w):
    d, n, a, b = w.shape
    return (jnp.eye(n, dtype=w.dtype)[None, :, None, :, None] * w[:, :, :, None, :]).reshape(d, n * a, n * b)


def _split_in_proj_kernel(w_ref, wa_ref, wgm_ref):
    rest = w_ref[:, MLA_Q_LORA + MLA_KV_LORA + MLA_ROPE:]
    g_mla, lru_u, g_lru, r_q, r_k, r_v, g_ret = (rest[:, n * WIDTH:(n + 1) * WIDTH] for n in range(7))
    merge = rest[:, 7 * WIDTH:]
    wgm_ref[...] = (0.5 * jnp.concatenate([g_mla, merge], axis=1)).astype(BF16)
    tile = lambda t, p: t[:, p * 128:(p + 1) * 128]
    ret = [piece for p in range(HEADS // 2)
           for piece in (tile(r_q, p), RET_DK ** -0.5 * tile(r_k, p), tile(r_v, p), 0.5 * tile(g_ret, p))]
    wa_ref[...] = jnp.concatenate(ret + [lru_u, 0.5 * g_lru], axis=1).astype(BF16)


def _split_in_proj(w_in, rows=256):
    in_cols = w_in.shape[-1]
    return pl.pallas_call(
        _split_in_proj_kernel,
        out_shape=(jax.ShapeDtypeStruct((DEPTH, D_MODEL, 6 * WIDTH), BF16),
                   jax.ShapeDtypeStruct((DEPTH, D_MODEL, WIDTH + N_BRANCHES * D_MODEL), BF16)),
        grid=(DEPTH, D_MODEL // rows),
        in_specs=[pl.BlockSpec((None, rows, in_cols), lambda l, r: (l, r, 0))],
        out_specs=(pl.BlockSpec((None, rows, 6 * WIDTH), lambda l, r: (l, r, 0)),
                   pl.BlockSpec((None, rows, WIDTH + N_BRANCHES * D_MODEL), lambda l, r: (l, r, 0))),
        compiler_params=_params(("parallel", "parallel")),
        name="split_in_proj",
    )(w_in)


def _prepare_weights(w_in, mla_w_uq, mla_w_ukv, lru_w_r, lru_w_i):
    offs = [0]
    for s in IN_SIZES:
        offs.append(offs[-1] + s)
    w_a, w_gm = _split_in_proj(w_in)
    w_bf = w_in[:, :, :offs[3]].astype(BF16)
    col = lambda i: w_bf[:, :, offs[i]:offs[i + 1]]
    zeros = lambda n: jnp.zeros((DEPTH, D_MODEL, n), BF16)
    half = MLA_ROPE // 2
    pad = HEAD_PAD - MLA_NOPE - MLA_ROPE - half
    w_lat = jnp.concatenate([col(0), col(1), zeros(MLA_NOPE), col(2), col(2)[:, :, :half], zeros(pad)], axis=2)

    w_uq = mla_w_uq.astype(BF16).reshape(DEPTH, MLA_Q_LORA, HEADS, MLA_NOPE + MLA_ROPE)
    w_uq = jnp.concatenate([w_uq, w_uq[..., MLA_NOPE:MLA_NOPE + half],
                            jnp.zeros((DEPTH, MLA_Q_LORA, HEADS, pad), BF16)], axis=3)
    w_uq = w_uq.reshape(DEPTH, MLA_Q_LORA, HEADS * HEAD_PAD)
    w_ukv = mla_w_ukv.astype(BF16).reshape(DEPTH, MLA_KV_LORA, HEADS, MLA_NOPE + MLA_V)
    tile_pad = ((0, 0), (0, 0), (0, 0), (0, HEAD_PAD - MLA_NOPE))
    w_ukv = jnp.concatenate([jnp.pad(w_ukv[..., :MLA_NOPE], tile_pad).reshape(DEPTH, MLA_KV_LORA, HEADS * HEAD_PAD),
                             jnp.pad(w_ukv[..., MLA_NOPE:], tile_pad).reshape(DEPTH, MLA_KV_LORA, HEADS * HEAD_PAD)],
                            axis=2)
    w_gates = 0.5 * jnp.concatenate([_block_diag(lru_w_r.astype(BF16)), _block_diag(lru_w_i.astype(BF16))], axis=2)
    return w_a, w_gm, w_lat, w_uq, w_ukv, w_gates


def kernel(x, positions, w_in, b_merge, mla_q_norm, mla_w_uq, mla_kv_norm, mla_w_ukv, lru_conv_w, lru_conv_b,
           lru_w_r, lru_b_r, lru_w_i, lru_b_i, lru_lambda, ret_gn_g, w_branch, w_out, ln_g, ln_b):
    pos = jnp.broadcast_to(positions.astype(F32)[:, :, None], positions.shape + (128,))
    mla_cos, mla_sin, ret_cos, ret_sin = _rope_tables(pos, *_rope_patterns())
    decay, q_w, k_w, chunk_decay, avg = _retention_constants()

    w_a, w_gm, w_lat, w_uq, w_ukv, w_gates = _prepare_weights(w_in, mla_w_uq, mla_w_ukv, lru_w_r, lru_w_i)
    w_b = w_branch.astype(BF16)
    w_o = w_out.astype(BF16)
    row = lambda v: v[:, None, :]
    b_gates = row(0.5 * jnp.concatenate([lru_b_r, lru_b_i], axis=1))
    b_m = row(0.5 * b_merge)

    for l in range(DEPTH):
        q, k, v = _mla_proj(x, mla_cos, mla_sin, l, w_lat, row(mla_q_norm), row(mla_kv_norm), w_uq, w_ukv)
        o_mla = _attention(q, k, v)
        y_lru = _lru(x, l, w_a, lru_conv_w, row(lru_conv_b), w_gates, b_gates, row(lru_lambda))
        y_ret = _retention(x, ret_cos, ret_sin, l, w_a, decay, q_w, k_w, chunk_decay, row(ret_gn_g), avg)
        x = _merge(x, o_mla, y_lru, y_ret, l, w_gm, b_m, w_b, w_o, row(ln_g), row(ln_b))
    return x
```

```python
import functools

import jax
import jax.numpy as jnp
import numpy as np
from jax import lax
from jax.experimental import pallas as pl
from jax.experimental.pallas import tpu as pltpu

F32 = jnp.float32
BF16 = jnp.bfloat16

D_MODEL = 1024
DEPTH = 2
HEADS = 8
MLA_Q_LORA = 384
MLA_KV_LORA = 256
MLA_NOPE = 64
MLA_ROPE = 32
MLA_V = 64
HEAD_PAD = 128
WIDTH = 512
LRU_CONV = 4
LRU_C = 8.0
LRU_ROWS = 256
MERGE_ROWS = 256
PROJ_ROWS = 256
RET_DK = 64
RET_CHUNK = 128
N_BRANCHES = 3
ROPE_BASE = 10000.0
LN_EPS = 1e-5
RMS_EPS = 1e-6
ALPHA = (2 * DEPTH) ** 0.25
ATTN_SCALE = (MLA_NOPE + MLA_ROPE) ** -0.5
NEG = -1e30
LOG2E = 1.4426950408889634

IN_SIZES = (MLA_Q_LORA, MLA_KV_LORA, MLA_ROPE, WIDTH, WIDTH, WIDTH, WIDTH, WIDTH, WIDTH, WIDTH,
            N_BRANCHES * D_MODEL)
LAT_COLS = MLA_Q_LORA + MLA_KV_LORA + HEAD_PAD

VMEM_LIMIT = 56 * 1024 * 1024


def _const_spec(shape):
    return pl.BlockSpec(shape, lambda *_: (0,) * len(shape))


def _layer_spec(arr, layer, width=None, block=0):
    shape = arr.shape[1:]
    if width is not None:
        shape = shape[:-1] + (width,)
    return pl.BlockSpec((None,) + shape, lambda *_: (layer,) + (0,) * (len(shape) - 1) + (block,))


def _params(sem, vmem=VMEM_LIMIT):
    return pltpu.CompilerParams(dimension_semantics=sem, vmem_limit_bytes=vmem)


def _dot(a, b):
    return jnp.dot(a, b, preferred_element_type=F32)


def _dot_nt(a, b):
    return lax.dot_general(a, b, (((1,), (1,)), ((), ())), preferred_element_type=F32)


def _dot_tn(a, b):
    return lax.dot_general(a, b, (((0,), (0,)), ((), ())), preferred_element_type=F32)


def _rope_table_kernel(pos_ref, freq_ref, msgn_ref, mbase_ref, rsgn_ref, mcos_ref, msin_ref, rcos_ref, rsin_ref):
    ang = pos_ref[0] * freq_ref[...]
    lane = lax.broadcasted_iota(jnp.int32, ang.shape, 1)
    first_half = (lane >= MLA_NOPE) & (lane < MLA_NOPE + MLA_ROPE // 2)
    second_half = (lane >= MLA_NOPE + MLA_ROPE // 2) & (lane < MLA_NOPE + MLA_ROPE)
    for trig, sgn_m, sgn_r, base, m_ref, r_ref in ((jnp.cos(ang), None, None, mbase_ref[...], mcos_ref, rcos_ref),
                                                   (jnp.sin(ang), msgn_ref[...], rsgn_ref[...], 0.0, msin_ref, rsin_ref)):
        mla = jnp.where(first_half, pltpu.roll(trig, 32, 1), jnp.where(second_half, pltpu.roll(trig, 48, 1), base))
        ret = jnp.where(lane < 32, trig, pltpu.roll(trig, 32, 1))
        ret = jnp.where(lane < 64, ret, pltpu.roll(ret, 64, 1))
        m_ref[0] = mla if sgn_m is None else mla * sgn_m
        r_ref[0] = ret if sgn_r is None else ret * sgn_r


def _rope_tables(positions, freq, mla_sgn, mla_base, ret_sgn, ts=1024):
    B, S, _ = positions.shape
    tab = jax.ShapeDtypeStruct((B, S, 128), F32)
    tok = pl.BlockSpec((1, ts, 128), lambda b, s: (b, s, 0))
    return pl.pallas_call(
        _rope_table_kernel,
        out_shape=(tab, tab, tab, tab),
        grid=(B, S // ts),
        in_specs=[tok] + [_const_spec((1, 128))] * 4,
        out_specs=(tok, tok, tok, tok),
        compiler_params=_params(("parallel", "parallel")),
        name="rope_tables",
    )(positions, freq, mla_sgn, mla_base, ret_sgn)


def _rope_lanes(x, cos, sin, lo_mask, half):
    swapped = jnp.where(lo_mask, pltpu.roll(x, 128 - half, 1), pltpu.roll(x, half, 1))
    return x * cos + swapped * sin


def _mla_proj_kernel(x_ref, cos_ref, sin_ref, wlat_ref, qn_ref, kvn_ref, wuq_ref, wukv_ref,
                     q_ref, k_ref, v_ref):
    tm = x_ref.shape[1]
    blocks = [slice(r, r + PROJ_ROWS) for r in range(0, tm, PROJ_ROWS)]

    def rms(v, g):
        return v * lax.rsqrt(jnp.mean(jnp.square(v), -1, keepdims=True) + RMS_EPS) * g

    def rotate(t, c, s):
        return t * c + pltpu.roll(t, 128 - MLA_ROPE // 2, 1) * s

    lat = [_dot(x_ref[0, rs, :].astype(BF16), wlat_ref[...]) for rs in blocks]
    q = [_dot(rms(t[:, :MLA_Q_LORA], qn_ref[...]).astype(BF16), wuq_ref[...]) for t in lat]
    kv = [_dot(rms(t[:, MLA_Q_LORA:MLA_Q_LORA + MLA_KV_LORA], kvn_ref[...]).astype(BF16), wukv_ref[...])
          for t in lat]
    lane = lax.broadcasted_iota(jnp.int32, (PROJ_ROWS, HEAD_PAD), 1)
    ones_lane = jnp.where(lane == MLA_V, 1.0, 0.0)
    for r, rs in enumerate(blocks):
        cos = cos_ref[0, rs, :]
        sin = sin_ref[0, rs, :]
        cos_q = cos * (ATTN_SCALE * LOG2E)
        sin_q = sin * (ATTN_SCALE * LOG2E)
        k_rot = rotate(lat[r][:, MLA_Q_LORA + MLA_KV_LORA:], cos, sin)
        for h in range(HEADS):
            sl = slice(h * HEAD_PAD, (h + 1) * HEAD_PAD)
            vl = slice((HEADS + h) * HEAD_PAD, (HEADS + h + 1) * HEAD_PAD)
            q_ref[0, rs, sl] = rotate(q[r][:, sl], cos_q, sin_q).astype(BF16)
            k_ref[0, rs, sl] = (kv[r][:, sl] + k_rot).astype(BF16)
            v_ref[0, rs, sl] = (kv[r][:, vl] + ones_lane).astype(BF16)


def _mla_proj(x, cos, sin, layer, wlat, qn, kvn, wuq, wukv, tm=512):
    B, S, _ = x.shape
    tok = lambda c: pl.BlockSpec((1, tm, c), lambda b, s: (b, s, 0))
    out = jax.ShapeDtypeStruct((B, S, HEADS * HEAD_PAD), BF16)
    return pl.pallas_call(
        _mla_proj_kernel,
        out_shape=(out, out, out),
        grid=(B, S // tm),
        in_specs=[tok(D_MODEL), tok(128), tok(128),
                  _layer_spec(wlat, layer),
                  _layer_spec(qn, layer), _layer_spec(kvn, layer), _layer_spec(wuq, layer), _layer_spec(wukv, layer)],
        out_specs=(tok(HEADS * HEAD_PAD), tok(HEADS * HEAD_PAD), tok(HEADS * HEAD_PAD)),
        compiler_params=_params(("parallel", "parallel")),
        name="mla_proj",
    )(x, cos, sin, wlat, qn, kvn, wuq, wukv)


def _attn_kernel(q_ref, k_ref, v_ref, o_ref, m_s, acc_s, s_a, s_b, *, tq, tk):
    i = pl.program_id(2)
    ratio = tq // tk
    assert ratio == 2
    m_s[...] = jnp.full(m_s.shape, NEG, F32)
    acc_s[...] = jnp.zeros(acc_s.shape, F32)
    heads = [slice(hh * HEAD_PAD, (hh + 1) * HEAD_PAD) for hh in range(2)]

    top, bottom = slice(0, tk), slice(tk, tq)

    def scores(j, hh, rows):
        start = pl.multiple_of(j * tk, tk)
        return _dot_nt(q_ref[0, rows, heads[hh]], k_ref[0, pl.ds(start, tk), heads[hh]])

    def accumulate(j, hh, s, rows):
        start = pl.multiple_of(j * tk, tk)
        m = m_s[hh, rows, :]
        m_new = jnp.maximum(m, jnp.max(s, -1, keepdims=True))
        p = jnp.concatenate([jnp.exp2(s[:, c * 128:(c + 1) * 128] - m_new) for c in range(tk // 128)], axis=1)
        pv = _dot(p.astype(BF16), v_ref[0, pl.ds(start, tk), heads[hh]])
        acc_s[hh, rows, :] = jnp.exp2(m - m_new) * acc_s[hh, rows, :] + pv
        m_s[hh, rows, :] = m_new

    for rows in (top, bottom):
        for hh in range(2):
            s_a[hh, rows, :] = scores(0, hh, rows)

    def body(n, _):
        for cur, nxt, j in ((s_a, s_b, 2 * n), (s_b, s_a, 2 * n + 1)):
            for rows in (top, bottom):
                for hh in range(2):
                    nxt[hh, rows, :] = scores(j + 1, hh, rows)
                    accumulate(j, hh, cur[hh, rows, :], rows)
        return 0

    lax.fori_loop(0, i, body, 0)
    causal = (lax.broadcasted_iota(jnp.int32, (tk, tk), 1) <= lax.broadcasted_iota(jnp.int32, (tk, tk), 0))
    for hh in range(2):
        s_b[hh, bottom, :] = scores(2 * i + 1, hh, bottom)
        accumulate(2 * i, hh, jnp.where(causal, s_a[hh, top, :], NEG), top)
        accumulate(2 * i, hh, s_a[hh, bottom, :], bottom)
        accumulate(2 * i + 1, hh, jnp.where(causal, s_b[hh, bottom, :], NEG), bottom)
    for hh in range(2):
        acc = acc_s[hh]
        o = acc[:, :MLA_V] / acc[:, MLA_V:MLA_V + 1]
        o_ref[0, :, hh * MLA_V:(hh + 1) * MLA_V] = o.astype(o_ref.dtype)


def _attention(q, k, v, tq=1024, tk=512):
    B, S, _ = q.shape
    return pl.pallas_call(
        functools.partial(_attn_kernel, tq=tq, tk=tk),
        out_shape=jax.ShapeDtypeStruct((B, S, WIDTH), BF16),
        grid=(B, HEADS // 2, S // tq),
        in_specs=[pl.BlockSpec((1, tq, 2 * HEAD_PAD), lambda b, h, i: (b, i, h)),
                  pl.BlockSpec((1, S, 2 * HEAD_PAD), lambda b, h, i: (b, 0, h)),
                  pl.BlockSpec((1, S, 2 * HEAD_PAD), lambda b, h, i: (b, 0, h))],
        out_specs=pl.BlockSpec((1, tq, 2 * MLA_V), lambda b, h, i: (b, i, h)),
        scratch_shapes=[pltpu.VMEM((2, tq, HEAD_PAD), F32), pltpu.VMEM((2, tq, HEAD_PAD), F32),
                        pltpu.VMEM((2, tq, tk), F32), pltpu.VMEM((2, tq, tk), F32)],
        compiler_params=_params(("parallel", "parallel", "arbitrary")),
        name="mla_attention",
    )(q, k, v)


def _lru_kernel(x_ref, w_ref, cw_ref, cb_ref, wg_ref, bg_ref, lam_ref, y_ref, ubuf, hcar, h_s, *, ts):
    @pl.when(pl.program_id(1) == 0)
    def _():
        ubuf[0:8, :] = jnp.zeros((8, WIDTH), F32)
        hcar[...] = jnp.zeros_like(hcar)

    nb = ts // LRU_ROWS
    blocks = [slice(r * LRU_ROWS, (r + 1) * LRU_ROWS) for r in range(nb)]
    z = [_dot(x_ref[0, rs, :].astype(BF16), w_ref[...]) for rs in blocks]
    xc = []
    for r, rs in enumerate(blocks):
        u = z[r][:, :WIDTH]
        lo = 8 + r * LRU_ROWS
        ubuf[lo:lo + LRU_ROWS, :] = u
        acc = cb_ref[...] + cw_ref[3:4, :] * u
        for j in range(1, LRU_CONV):
            acc = acc + cw_ref[3 - j:4 - j, :] * ubuf[lo - j:lo - j + LRU_ROWS, :]
        xc.append(acc)
    ubuf[0:8, :] = z[nb - 1][LRU_ROWS - 8:, :WIDTH]

    th = [jnp.tanh(_dot(xc[r].astype(BF16), wg_ref[...]) + bg_ref[...]) for r in range(nb)]
    lam = lam_ref[...]
    softplus_neg = jnp.maximum(-lam, 0.0) + jnp.log1p(jnp.exp(-jnp.abs(lam)))
    c1 = (-0.5 * LRU_C) * softplus_neg
    groups = LRU_ROWS // 8
    sub = lax.broadcasted_iota(jnp.int32, (groups, 8, WIDTH), 1)
    a3, b3 = [], []
    for r in range(nb):
        log_a = th[r][:, :WIDTH] * c1 + c1
        a = jnp.exp(log_a)
        t = jnp.tanh(log_a)
        b_sq = -t * (1.0 + a * a)
        root = jnp.where(b_sq > 0.0, b_sq * lax.rsqrt(b_sq), 0.0)
        b = root * ((th[r][:, WIDTH:] + 1.0) * (0.5 * xc[r]))
        a = a.reshape(groups, 8, WIDTH)
        b = b.reshape(groups, 8, WIDTH)
        for d in (1, 2, 4):
            keep = sub >= d
            b = a * jnp.where(keep, pltpu.roll(b, d, 1), 0.0) + b
            a = a * jnp.where(keep, pltpu.roll(a, d, 1), 1.0)
        a3.append(a)
        b3.append(b)
    carry = hcar[...]
    for r in range(nb):
        for n in range(groups):
            h = a3[r][n] * carry + b3[r][n]
            h_s[r * LRU_ROWS + n * 8:r * LRU_ROWS + (n + 1) * 8, :] = h
            carry = jnp.broadcast_to(h[7:8, :], (8, WIDTH))
    hcar[...] = carry
    for r, rs in enumerate(blocks):
        half_g = z[r][:, WIDTH:]
        y_ref[0, rs, :] = (h_s[rs, :] * (half_g * jnp.tanh(half_g) + half_g)).astype(y_ref.dtype)


def _lru(x, layer, w_a, cw, cb, wg, bg, lam, ts=512):
    B, S, _ = x.shape
    return pl.pallas_call(
        functools.partial(_lru_kernel, ts=ts),
        out_shape=jax.ShapeDtypeStruct((B, S, WIDTH), BF16),
        grid=(B, S // ts),
        in_specs=[pl.BlockSpec((1, ts, D_MODEL), lambda b, s: (b, s, 0)),
                  _layer_spec(w_a, layer, 2 * WIDTH, 2), _layer_spec(cw, layer), _layer_spec(cb, layer),
                  _layer_spec(wg, layer), _layer_spec(bg, layer), _layer_spec(lam, layer)],
        out_specs=pl.BlockSpec((1, ts, WIDTH), lambda b, s: (b, s, 0)),
        scratch_shapes=[pltpu.VMEM((ts + 8, WIDTH), F32), pltpu.VMEM((8, WIDTH), F32),
                        pltpu.VMEM((ts, WIDTH), F32)],
        compiler_params=_params(("parallel", "arbitrary")),
        name="rglru",
    )(x, w_a, cw, cb, wg, bg, lam)


def _ret_kernel(x_ref, cos_ref, sin_ref, w_ref, decay_ref, qw_ref, kw_ref, cd_ref, gn_ref, avg_ref, y_ref,
                state, *, ts):
    C = RET_CHUNK
    n_chunks = ts // C

    @pl.when(pl.program_id(1) == 0)
    def _():
        state[...] = jnp.zeros_like(state)

    xb = x_ref[0].astype(BF16)
    cos = cos_ref[0]
    sin = sin_ref[0]
    lo = (lax.broadcasted_iota(jnp.int32, cos.shape, 1) % RET_DK) < RET_DK // 2
    first = lax.broadcasted_iota(jnp.int32, (C, 128), 1) < RET_DK
    same_head = (lax.broadcasted_iota(jnp.int32, (128, 128), 0) < RET_DK) == (
        lax.broadcasted_iota(jnp.int32, (128, 128), 1) < RET_DK)

    def group_mean(val):
        hi = val.astype(BF16)
        lo_part = (val - hi.astype(F32)).astype(BF16)
        return _dot(jnp.concatenate([hi, lo_part], axis=1), avg_ref[...])

    pairs = range(HEADS // 2)
    chunks = range(n_chunks)
    tiles = [(p, c) for p in pairs for c in chunks]
    lanes = [slice(p * 128, (p + 1) * 128) for p in pairs]
    z = _dot(xb, w_ref[...])
    qb = [_rope_lanes(z[:, p * 512:p * 512 + 128], cos, sin, lo, RET_DK // 2) for p in pairs]
    kb = [_rope_lanes(z[:, p * 512 + 128:p * 512 + 256], cos, sin, lo, RET_DK // 2) for p in pairs]
    vb = [z[:, p * 512 + 256:p * 512 + 384].astype(BF16) for p in pairs]
    rows = [slice(c * C, (c + 1) * C) for c in chunks]
    q2 = {(p, c): jnp.concatenate([jnp.where(first, qb[p][rows[c]], 0.0), jnp.where(first, 0.0, qb[p][rows[c]])],
                                  axis=0).astype(BF16) for p, c in tiles}
    scores = {(p, c): _dot_nt(q2[p, c], kb[p][rows[c]].astype(BF16)) * decay_ref[p] for p, c in tiles}
    inner2 = {(p, c): _dot(scores[p, c].astype(BF16), vb[p][rows[c]]) for p, c in tiles}
    inner = {t: jnp.where(first, inner2[t][:C], inner2[t][C:]) for t in tiles}
    kv = {(p, c): jnp.where(same_head, _dot_tn((kb[p][rows[c]] * kw_ref[:, lanes[p]]).astype(BF16), vb[p][rows[c]]),
                            0.0) for p, c in tiles}
    q_weighted = {(p, c): (qb[p][rows[c]] * qw_ref[:, lanes[p]]).astype(BF16) for p, c in tiles}
    s_prev = [state[p] for p in pairs]
    outs = {}
    for c in chunks:
        for p in pairs:
            outs[p, c] = inner[p, c] + _dot(q_weighted[p, c], s_prev[p].astype(BF16))
            s_prev[p] = cd_ref[:, lanes[p]] * s_prev[p] + kv[p, c]
    for p in pairs:
        state[p] = s_prev[p]
    o = [jnp.concatenate([outs[p, c] for c in chunks], axis=0) for p in pairs]
    mean = [group_mean(o[p]) for p in pairs]
    dev = [o[p] - mean[p] for p in pairs]
    var = [group_mean(dev[p] * dev[p]) for p in pairs]
    for p in pairs:
        half_g = z[:, p * 512 + 384:(p + 1) * 512]
        y = dev[p] * lax.rsqrt(var[p] + LN_EPS) * gn_ref[:, lanes[p]] * (half_g * jnp.tanh(half_g) + half_g)
        y_ref[0, :, lanes[p]] = y.astype(y_ref.dtype)


def _retention(x, cos, sin, layer, w_a, decay, qw, kw, cd, gn, avg, ts=512):
    B, S, _ = x.shape
    tok = lambda c: pl.BlockSpec((1, ts, c), lambda b, s: (b, s, 0))
    return pl.pallas_call(
        functools.partial(_ret_kernel, ts=ts),
        out_shape=jax.ShapeDtypeStruct((B, S, WIDTH), BF16),
        grid=(B, S // ts),
        in_specs=[tok(D_MODEL), tok(128), tok(128), _layer_spec(w_a, layer, 4 * WIDTH, 0),
                  _const_spec(decay.shape), _const_spec(qw.shape), _const_spec(kw.shape), _const_spec(cd.shape),
                  _layer_spec(gn, layer), _const_spec(avg.shape)],
        out_specs=tok(WIDTH),
        scratch_shapes=[pltpu.VMEM((HEADS // 2, 128, 128), F32)],
        compiler_params=_params(("parallel", "arbitrary")),
        name="retention",
    )(x, cos, sin, w_a, decay, qw, kw, cd, gn, avg)


def _merge_kernel(x_ref, o_ref, yl_ref, yr_ref, wg_ref, bm_ref, wb_ref, wo_ref, lg_ref, lb_ref, out_ref):
    tm = x_ref.shape[1]
    blocks = [slice(r, r + MERGE_ROWS) for r in range(0, tm, MERGE_ROWS)]
    zg = [_dot(x_ref[0, rs, :].astype(BF16), wg_ref[...]) for rs in blocks]
    mixed = []
    for r, rs in enumerate(blocks):
        half_g = zg[r][:, :WIDTH]
        y_mla = (o_ref[0, rs, :].astype(F32) * (half_g * jnp.tanh(half_g) + half_g)).astype(BF16)
        ys = (y_mla, yl_ref[0, rs, :], yr_ref[0, rs, :])
        total = None
        for n in range(N_BRANCHES):
            lo = WIDTH + n * D_MODEL
            gate = 0.5 * jnp.tanh(zg[r][:, lo:lo + D_MODEL] + bm_ref[:, n * D_MODEL:(n + 1) * D_MODEL]) + 0.5
            term = gate * _dot(ys[n], wb_ref[n])
            total = term if total is None else total + term
        mixed.append(total.astype(BF16))
    proj = [_dot(mixed[r], wo_ref[...]) for r in range(len(blocks))]
    for r, rs in enumerate(blocks):
        xf = ALPHA * x_ref[0, rs, :] + proj[r]
        mu = jnp.mean(xf, -1, keepdims=True)
        var = jnp.mean(jnp.square(xf - mu), -1, keepdims=True)
        out_ref[0, rs, :] = (xf - mu) * lax.rsqrt(var + LN_EPS) * lg_ref[...] + lb_ref[...]


def _merge(x, o, yl, yr, layer, wg, bm, wb, wo, lg, lb, tm=512):
    B, S, _ = x.shape
    tok = lambda c: pl.BlockSpec((1, tm, c), lambda b, s: (b, s, 0))
    return pl.pallas_call(
        _merge_kernel,
        out_shape=jax.ShapeDtypeStruct((B, S, D_MODEL), F32),
        grid=(B, S // tm),
        in_specs=[tok(D_MODEL), tok(WIDTH), tok(WIDTH), tok(WIDTH),
                  _layer_spec(wg, layer), _layer_spec(bm, layer), _layer_spec(wb, layer),
                  _layer_spec(wo, layer), _layer_spec(lg, layer), _layer_spec(lb, layer)],
        out_specs=tok(D_MODEL),
        compiler_params=_params(("parallel", "parallel")),
        name="merge_out_ln",
    )(x, o, yl, yr, wg, bm, wb, wo, lg, lb)


def _rope_patterns():
    lane = np.arange(128)
    f32 = lambda a: np.asarray(a, np.float32)[None, :]
    half = MLA_ROPE // 2
    mla_inv = ROPE_BASE ** (-np.arange(half, dtype=np.float64) / half)
    in_rope = (lane >= MLA_NOPE) & (lane < MLA_NOPE + MLA_ROPE)
    mla_sgn = f32(np.where(in_rope, np.where(lane < MLA_NOPE + half, -1.0, 1.0), 0.0))
    mla_base = f32(np.where(lane < MLA_NOPE, 1.0, 0.0))
    half = RET_DK // 2
    ret_inv = ROPE_BASE ** (-np.arange(half, dtype=np.float64) / half)
    ret_sgn = f32(np.where((lane % RET_DK) < half, -1.0, 1.0))
    freq = f32(np.concatenate([ret_inv, mla_inv, np.zeros(128 - ret_inv.size - mla_inv.size)]))
    return freq, mla_sgn, mla_base, ret_sgn


def _retention_constants():
    C = RET_CHUNK
    log_gamma = np.log1p(-np.exp2(-5.0 - np.arange(HEADS, dtype=np.float64)))
    idx = np.arange(C, dtype=np.float64)
    diff = idx[:, None] - idx[None, :]
    decay = np.where(diff[None] >= 0, np.exp(diff[None] * log_gamma[:, None, None]), 0.0)
    k_w = np.exp((C - 1.0 - idx)[:, None] * log_gamma[None, :])
    q_w = np.exp((idx + 1.0)[:, None] * log_gamma[None, :])
    chunk_decay = np.exp(C * log_gamma)
    expand = lambda t: np.repeat(t, RET_DK, axis=-1).astype(np.float32)
    lane_head = np.arange(128) // RET_DK
    avg = np.where(lane_head[:, None] == lane_head[None, :], 1.0 / RET_DK, 0.0)
    avg = jnp.asarray(np.concatenate([avg, avg], axis=0), BF16)
    return (decay.reshape(HEADS // 2, 2 * C, C).astype(np.float32), expand(q_w), expand(k_w),
            expand(chunk_decay[None, :]), avg)


def _block_diag(w):
    d, n, a, b = w.shape
    return (jnp.eye(n, dtype=w.dtype)[None, :, None, :, None] * w[:, :, :, None, :]).reshape(d, n * a, n * b)


def _split_in_proj_kernel(wt_ref, wa_ref, wgm_ref, wlat_ref):
    def put(dst_ref, dst_col, src_row, scale=None):
        t = wt_ref[src_row:src_row + 128, :].T
        dst_ref[:, dst_col:dst_col + 128] = (t if scale is None else scale * t).astype(BF16)

    lat_end = MLA_Q_LORA + MLA_KV_LORA
    base = lat_end + MLA_ROPE
    g_mla, lru_u, g_lru, r_q, r_k, r_v, g_ret, merge = (base + n * WIDTH for n in range(8))
    for c in range(WIDTH // 128):
        put(wgm_ref, c * 128, g_mla + c * 128, 0.5)
        put(wa_ref, 4 * WIDTH + c * 128, lru_u + c * 128)
        put(wa_ref, 5 * WIDTH + c * 128, g_lru + c * 128, 0.5)
    for c in range(N_BRANCHES * D_MODEL // 128):
        put(wgm_ref, WIDTH + c * 128, merge + c * 128, 0.5)
    for p in range(HEADS // 2):
        for n, (src, scale) in enumerate(((r_q, None), (r_k, RET_DK ** -0.5), (r_v, None), (g_ret, 0.5))):
            put(wa_ref, p * 4 * 128 + n * 128, src + p * 128, scale)
    for c in range(lat_end // 128):
        put(wlat_ref, c * 128, c * 128)
    half = MLA_ROPE // 2
    lanes = wt_ref.shape[1]
    k_pe = jnp.concatenate([jnp.zeros((MLA_NOPE, lanes), F32), wt_ref[lat_end:base, :], wt_ref[lat_end:lat_end + half, :],
                            jnp.zeros((HEAD_PAD - MLA_NOPE - MLA_ROPE - half, lanes), F32)], axis=0)
    wlat_ref[:, lat_end:lat_end + HEAD_PAD] = k_pe.T.astype(BF16)


def _split_in_proj(w_in, lanes=256):
    w_t = jnp.swapaxes(w_in, 1, 2)
    in_cols = w_t.shape[1]
    out = lambda n: jax.ShapeDtypeStruct((DEPTH, D_MODEL, n), BF16)
    spec = lambda n: pl.BlockSpec((None, lanes, n), lambda l, r: (l, r, 0))
    widths = (6 * WIDTH, WIDTH + N_BRANCHES * D_MODEL, LAT_COLS)
    return pl.pallas_call(
        _split_in_proj_kernel,
        out_shape=tuple(out(n) for n in widths),
        grid=(DEPTH, D_MODEL // lanes),
        in_specs=[pl.BlockSpec((None, in_cols, lanes), lambda l, r: (l, 0, r))],
        out_specs=tuple(spec(n) for n in widths),
        compiler_params=_params(("parallel", "parallel")),
        name="split_in_proj",
    )(w_t)


def _prepare_weights(w_in, mla_w_uq, mla_w_ukv, lru_w_r, lru_w_i):
    assert w_in.shape[-1] == sum(IN_SIZES)
    w_a, w_gm, w_lat = _split_in_proj(w_in)
    half = MLA_ROPE // 2
    pad = HEAD_PAD - MLA_NOPE - MLA_ROPE - half

    w_uq = mla_w_uq.astype(BF16).reshape(DEPTH, MLA_Q_LORA, HEADS, MLA_NOPE + MLA_ROPE)
    w_uq = jnp.concatenate([w_uq, w_uq[..., MLA_NOPE:MLA_NOPE + half],
                            jnp.zeros((DEPTH, MLA_Q_LORA, HEADS, pad), BF16)], axis=3)
    w_uq = w_uq.reshape(DEPTH, MLA_Q_LORA, HEADS * HEAD_PAD)
    w_ukv = mla_w_ukv.astype(BF16).reshape(DEPTH, MLA_KV_LORA, HEADS, MLA_NOPE + MLA_V)
    tile_pad = ((0, 0), (0, 0), (0, 0), (0, HEAD_PAD - MLA_NOPE))
    w_ukv = jnp.concatenate([jnp.pad(w_ukv[..., :MLA_NOPE], tile_pad).reshape(DEPTH, MLA_KV_LORA, HEADS * HEAD_PAD),
                             jnp.pad(w_ukv[..., MLA_NOPE:], tile_pad).reshape(DEPTH, MLA_KV_LORA, HEADS * HEAD_PAD)],
                            axis=2)
    w_gates = 0.5 * jnp.concatenate([_block_diag(lru_w_r.astype(BF16)), _block_diag(lru_w_i.astype(BF16))], axis=2)
    return w_a, w_gm, w_lat, w_uq, w_ukv, w_gates


def kernel(x, positions, w_in, b_merge, mla_q_norm, mla_w_uq, mla_kv_norm, mla_w_ukv, lru_conv_w, lru_conv_b,
           lru_w_r, lru_b_r, lru_w_i, lru_b_i, lru_lambda, ret_gn_g, w_branch, w_out, ln_g, ln_b):
    pos = jnp.broadcast_to(positions.astype(F32)[:, :, None], positions.shape + (128,))
    mla_cos, mla_sin, ret_cos, ret_sin = _rope_tables(pos, *_rope_patterns())
    decay, q_w, k_w, chunk_decay, avg = _retention_constants()

    w_a, w_gm, w_lat, w_uq, w_ukv, w_gates = _prepare_weights(w_in, mla_w_uq, mla_w_ukv, lru_w_r, lru_w_i)
    w_b = w_branch.astype(BF16)
    w_o = w_out.astype(BF16)
    row = lambda v: v[:, None, :]
    b_gates = row(0.5 * jnp.concatenate([lru_b_r, lru_b_i], axis=1))
    b_m = row(0.5 * b_merge)

    for l in range(DEPTH):
        q, k, v = _mla_proj(x, mla_cos, mla_sin, l, w_lat, row(mla_q_norm), row(mla_kv_norm), w_uq, w_ukv)
        o_mla = _attention(q, k, v)
        y_lru = _lru(x, l, w_a, lru_conv_w, row(lru_conv_b), w_gates, b_gates, row(lru_lambda))
        y_ret = _retention(x, ret_cos, ret_sin, l, w_a, decay, q_w, k_w, chunk_decay, row(ret_gn_g), avg)
        x = _merge(x, o_mla, y_lru, y_ret, l, w_gm, b_m, w_b, w_o, row(ln_g), row(ln_b))
    return x
```

```python
import functools

import jax
import jax.numpy as jnp
import numpy as np
from jax import lax
from jax.experimental import pallas as pl
from jax.experimental.pallas import tpu as pltpu

F32 = jnp.float32
BF16 = jnp.bfloat16

D_MODEL = 1024
DEPTH = 2
HEADS = 8
MLA_Q_LORA = 384
MLA_KV_LORA = 256
MLA_NOPE = 64
MLA_ROPE = 32
MLA_V = 64
HEAD_PAD = 128
WIDTH = 512
LRU_CONV = 4
LRU_C = 8.0
LRU_ROWS = 256
MERGE_ROWS = 256
PROJ_ROWS = 256
RET_DK = 64
RET_CHUNK = 128
N_BRANCHES = 3
ROPE_BASE = 10000.0
LN_EPS = 1e-5
RMS_EPS = 1e-6
ALPHA = (2 * DEPTH) ** 0.25
ATTN_SCALE = (MLA_NOPE + MLA_ROPE) ** -0.5
NEG = -1e30
LOG2E = 1.4426950408889634

IN_SIZES = (MLA_Q_LORA, MLA_KV_LORA, MLA_ROPE, WIDTH, WIDTH, WIDTH, WIDTH, WIDTH, WIDTH, WIDTH,
            N_BRANCHES * D_MODEL)
LAT_COLS = MLA_Q_LORA + MLA_KV_LORA + HEAD_PAD

VMEM_LIMIT = 56 * 1024 * 1024


def _const_spec(shape):
    return pl.BlockSpec(shape, lambda *_: (0,) * len(shape))


def _layer_spec(arr, layer, width=None, block=0):
    shape = arr.shape[1:]
    if width is not None:
        shape = shape[:-1] + (width,)
    return pl.BlockSpec((None,) + shape, lambda *_: (layer,) + (0,) * (len(shape) - 1) + (block,))


def _params(sem, vmem=VMEM_LIMIT):
    return pltpu.CompilerParams(dimension_semantics=sem, vmem_limit_bytes=vmem)


def _dot(a, b):
    return jnp.dot(a, b, preferred_element_type=F32)


def _dot_nt(a, b):
    return lax.dot_general(a, b, (((1,), (1,)), ((), ())), preferred_element_type=F32)


def _dot_tn(a, b):
    return lax.dot_general(a, b, (((0,), (0,)), ((), ())), preferred_element_type=F32)


def _rope_table_kernel(pos_ref, freq_ref, msgn_ref, mbase_ref, rsgn_ref, mcos_ref, msin_ref, rcos_ref, rsin_ref):
    ang = pos_ref[0] * freq_ref[...]
    lane = lax.broadcasted_iota(jnp.int32, ang.shape, 1)
    first_half = (lane >= MLA_NOPE) & (lane < MLA_NOPE + MLA_ROPE // 2)
    second_half = (lane >= MLA_NOPE + MLA_ROPE // 2) & (lane < MLA_NOPE + MLA_ROPE)
    for trig, sgn_m, sgn_r, base, m_ref, r_ref in ((jnp.cos(ang), None, None, mbase_ref[...], mcos_ref, rcos_ref),
                                                   (jnp.sin(ang), msgn_ref[...], rsgn_ref[...], 0.0, msin_ref, rsin_ref)):
        mla = jnp.where(first_half, pltpu.roll(trig, 32, 1), jnp.where(second_half, pltpu.roll(trig, 48, 1), base))
        ret = jnp.where(lane < 32, trig, pltpu.roll(trig, 32, 1))
        ret = jnp.where(lane < 64, ret, pltpu.roll(ret, 64, 1))
        m_ref[0] = mla if sgn_m is None else mla * sgn_m
        r_ref[0] = ret if sgn_r is None else ret * sgn_r


def _rope_tables(positions, freq, mla_sgn, mla_base, ret_sgn, ts=1024):
    B, S, _ = positions.shape
    tab = jax.ShapeDtypeStruct((B, S, 128), F32)
    tok = pl.BlockSpec((1, ts, 128), lambda b, s: (b, s, 0))
    return pl.pallas_call(
        _rope_table_kernel,
        out_shape=(tab, tab, tab, tab),
        grid=(B, S // ts),
        in_specs=[tok] + [_const_spec((1, 128))] * 4,
        out_specs=(tok, tok, tok, tok),
        compiler_params=_params(("parallel", "parallel")),
        name="rope_tables",
    )(positions, freq, mla_sgn, mla_base, ret_sgn)


def _rope_lanes(x, cos, sin, lo_mask, half):
    swapped = jnp.where(lo_mask, pltpu.roll(x, 128 - half, 1), pltpu.roll(x, half, 1))
    return x * cos + swapped * sin


def _mla_proj_kernel(x_ref, cos_ref, sin_ref, wlat_ref, qn_ref, kvn_ref, wuq_ref, wukv_ref,
                     q_ref, k_ref, v_ref):
    tm = x_ref.shape[1]
    blocks = [slice(r, r + PROJ_ROWS) for r in range(0, tm, PROJ_ROWS)]

    def rms(v, g):
        return v * lax.rsqrt(jnp.mean(jnp.square(v), -1, keepdims=True) + RMS_EPS) * g

    def rotate(t, c, s):
        return t * c + pltpu.roll(t, 128 - MLA_ROPE // 2, 1) * s

    lat = [_dot(x_ref[0, rs, :].astype(BF16), wlat_ref[...]) for rs in blocks]
    q = [_dot(rms(t[:, :MLA_Q_LORA], qn_ref[...]).astype(BF16), wuq_ref[...]) for t in lat]
    kv = [_dot(rms(t[:, MLA_Q_LORA:MLA_Q_LORA + MLA_KV_LORA], kvn_ref[...]).astype(BF16), wukv_ref[...])
          for t in lat]
    lane = lax.broadcasted_iota(jnp.int32, (PROJ_ROWS, HEAD_PAD), 1)
    ones_lane = jnp.where(lane == MLA_V, 1.0, 0.0)
    for r, rs in enumerate(blocks):
        cos = cos_ref[0, rs, :]
        sin = sin_ref[0, rs, :]
        cos_q = cos * (ATTN_SCALE * LOG2E)
        sin_q = sin * (ATTN_SCALE * LOG2E)
        k_rot = rotate(lat[r][:, MLA_Q_LORA + MLA_KV_LORA:], cos, sin)
        for h in range(HEADS):
            sl = slice(h * HEAD_PAD, (h + 1) * HEAD_PAD)
            vl = slice((HEADS + h) * HEAD_PAD, (HEADS + h + 1) * HEAD_PAD)
            q_ref[0, rs, sl] = rotate(q[r][:, sl], cos_q, sin_q).astype(BF16)
            k_ref[0, rs, sl] = (kv[r][:, sl] + k_rot).astype(BF16)
            v_ref[0, rs, sl] = (kv[r][:, vl] + ones_lane).astype(BF16)


def _mla_proj(x, cos, sin, layer, wlat, qn, kvn, wuq, wukv, tm=1024):
    B, S, _ = x.shape
    tok = lambda c: pl.BlockSpec((1, tm, c), lambda b, s: (b, s, 0))
    out = jax.ShapeDtypeStruct((B, S, HEADS * HEAD_PAD), BF16)
    return pl.pallas_call(
        _mla_proj_kernel,
        out_shape=(out, out, out),
        grid=(B, S // tm),
        in_specs=[tok(D_MODEL), tok(128), tok(128),
                  _layer_spec(wlat, layer),
                  _layer_spec(qn, layer), _layer_spec(kvn, layer), _layer_spec(wuq, layer), _layer_spec(wukv, layer)],
        out_specs=(tok(HEADS * HEAD_PAD), tok(HEADS * HEAD_PAD), tok(HEADS * HEAD_PAD)),
        compiler_params=_params(("parallel", "parallel")),
        name="mla_proj",
    )(x, cos, sin, wlat, qn, kvn, wuq, wukv)


def _attn_kernel(q_ref, k_ref, v_ref, o_ref, m_s, acc_s, s_a, s_b, *, tq, tk):
    i = pl.program_id(2)
    ratio = tq // tk
    assert ratio == 2
    m_s[...] = jnp.full(m_s.shape, NEG, F32)
    acc_s[...] = jnp.zeros(acc_s.shape, F32)
    heads = [slice(hh * HEAD_PAD, (hh + 1) * HEAD_PAD) for hh in range(2)]

    top, bottom = slice(0, tk), slice(tk, tq)

    def scores(j, hh, rows):
        start = pl.multiple_of(j * tk, tk)
        return _dot_nt(q_ref[0, rows, heads[hh]], k_ref[0, pl.ds(start, tk), heads[hh]])

    def accumulate(j, hh, s, rows):
        start = pl.multiple_of(j * tk, tk)
        m = m_s[hh, rows, :]
        m_new = jnp.maximum(m, jnp.max(s, -1, keepdims=True))
        p = jnp.concatenate([jnp.exp2(s[:, c * 128:(c + 1) * 128] - m_new) for c in range(tk // 128)], axis=1)
        pv = _dot(p.astype(BF16), v_ref[0, pl.ds(start, tk), heads[hh]])
        acc_s[hh, rows, :] = jnp.exp2(m - m_new) * acc_s[hh, rows, :] + pv
        m_s[hh, rows, :] = m_new

    for rows in (top, bottom):
        for hh in range(2):
            s_a[hh, rows, :] = scores(0, hh, rows)

    def body(n, _):
        for cur, nxt, j in ((s_a, s_b, 2 * n), (s_b, s_a, 2 * n + 1)):
            for rows in (top, bottom):
                for hh in range(2):
                    nxt[hh, rows, :] = scores(j + 1, hh, rows)
                    accumulate(j, hh, cur[hh, rows, :], rows)
        return 0

    lax.fori_loop(0, i, body, 0)
    causal = (lax.broadcasted_iota(jnp.int32, (tk, tk), 1) <= lax.broadcasted_iota(jnp.int32, (tk, tk), 0))
    for hh in range(2):
        s_b[hh, bottom, :] = scores(2 * i + 1, hh, bottom)
        accumulate(2 * i, hh, jnp.where(causal, s_a[hh, top, :], NEG), top)
        accumulate(2 * i, hh, s_a[hh, bottom, :], bottom)
        accumulate(2 * i + 1, hh, jnp.where(causal, s_b[hh, bottom, :], NEG), bottom)
    for hh in range(2):
        acc = acc_s[hh]
        o = acc[:, :MLA_V] / acc[:, MLA_V:MLA_V + 1]
        o_ref[0, :, hh * MLA_V:(hh + 1) * MLA_V] = o.astype(o_ref.dtype)


def _attention(q, k, v, tq=1024, tk=512):
    B, S, _ = q.shape
    return pl.pallas_call(
        functools.partial(_attn_kernel, tq=tq, tk=tk),
        out_shape=jax.ShapeDtypeStruct((B, S, WIDTH), BF16),
        grid=(B, HEADS // 2, S // tq),
        in_specs=[pl.BlockSpec((1, tq, 2 * HEAD_PAD), lambda b, h, i: (b, i, h)),
                  pl.BlockSpec((1, S, 2 * HEAD_PAD), lambda b, h, i: (b, 0, h)),
                  pl.BlockSpec((1, S, 2 * HEAD_PAD), lambda b, h, i: (b, 0, h))],
        out_specs=pl.BlockSpec((1, tq, 2 * MLA_V), lambda b, h, i: (b, i, h)),
        scratch_shapes=[pltpu.VMEM((2, tq, HEAD_PAD), F32), pltpu.VMEM((2, tq, HEAD_PAD), F32),
                        pltpu.VMEM((2, tq, tk), F32), pltpu.VMEM((2, tq, tk), F32)],
        compiler_params=_params(("parallel", "parallel", "arbitrary")),
        name="mla_attention",
    )(q, k, v)


def _lru_kernel(x_ref, w_ref, cw_ref, cb_ref, wg_ref, bg_ref, lam_ref, y_ref, ubuf, hcar, h_s, *, ts):
    @pl.when(pl.program_id(1) == 0)
    def _():
        ubuf[0:8, :] = jnp.zeros((8, WIDTH), F32)
        hcar[...] = jnp.zeros_like(hcar)

    nb = ts // LRU_ROWS
    blocks = [slice(r * LRU_ROWS, (r + 1) * LRU_ROWS) for r in range(nb)]
    z = [_dot(x_ref[0, rs, :].astype(BF16), w_ref[...]) for rs in blocks]
    xc = []
    for r, rs in enumerate(blocks):
        u = z[r][:, :WIDTH]
        lo = 8 + r * LRU_ROWS
        ubuf[lo:lo + LRU_ROWS, :] = u
        acc = cb_ref[...] + cw_ref[3:4, :] * u
        for j in range(1, LRU_CONV):
            acc = acc + cw_ref[3 - j:4 - j, :] * ubuf[lo - j:lo - j + LRU_ROWS, :]
        xc.append(acc)
    ubuf[0:8, :] = z[nb - 1][LRU_ROWS - 8:, :WIDTH]

    th = [jnp.tanh(_dot(xc[r].astype(BF16), wg_ref[...]) + bg_ref[...]) for r in range(nb)]
    lam = lam_ref[...]
    softplus_neg = jnp.maximum(-lam, 0.0) + jnp.log1p(jnp.exp(-jnp.abs(lam)))
    c1 = (-0.5 * LRU_C) * softplus_neg
    groups = LRU_ROWS // 8
    sub = lax.broadcasted_iota(jnp.int32, (groups, 8, WIDTH), 1)
    a3, b3 = [], []
    for r in range(nb):
        log_a = th[r][:, :WIDTH] * c1 + c1
        a = jnp.exp(log_a)
        t = jnp.tanh(log_a)
        b_sq = -t * (1.0 + a * a)
        root = jnp.where(b_sq > 0.0, b_sq * lax.rsqrt(b_sq), 0.0)
        b = root * ((th[r][:, WIDTH:] + 1.0) * (0.5 * xc[r]))
        a = a.reshape(groups, 8, WIDTH)
        b = b.reshape(groups, 8, WIDTH)
        for d in (1, 2, 4):
            keep = sub >= d
            b = a * jnp.where(keep, pltpu.roll(b, d, 1), 0.0) + b
            a = a * jnp.where(keep, pltpu.roll(a, d, 1), 1.0)
        a3.append(a)
        b3.append(b)
    carry = hcar[...]
    for r in range(nb):
        for n in range(groups):
            h = a3[r][n] * carry + b3[r][n]
            h_s[r * LRU_ROWS + n * 8:r * LRU_ROWS + (n + 1) * 8, :] = h
            carry = jnp.broadcast_to(h[7:8, :], (8, WIDTH))
    hcar[...] = carry
    for r, rs in enumerate(blocks):
        half_g = z[r][:, WIDTH:]
        y_ref[0, rs, :] = (h_s[rs, :] * (half_g * jnp.tanh(half_g) + half_g)).astype(y_ref.dtype)


def _lru(x, layer, w_a, cw, cb, wg, bg, lam, ts=1024):
    B, S, _ = x.shape
    return pl.pallas_call(
        functools.partial(_lru_kernel, ts=ts),
        out_shape=jax.ShapeDtypeStruct((B, S, WIDTH), BF16),
        grid=(B, S // ts),
        in_specs=[pl.BlockSpec((1, ts, D_MODEL), lambda b, s: (b, s, 0)),
                  _layer_spec(w_a, layer, 2 * WIDTH, 2), _layer_spec(cw, layer), _layer_spec(cb, layer),
                  _layer_spec(wg, layer), _layer_spec(bg, layer), _layer_spec(lam, layer)],
        out_specs=pl.BlockSpec((1, ts, WIDTH), lambda b, s: (b, s, 0)),
        scratch_shapes=[pltpu.VMEM((ts + 8, WIDTH), F32), pltpu.VMEM((8, WIDTH), F32),
                        pltpu.VMEM((ts, WIDTH), F32)],
        compiler_params=_params(("parallel", "arbitrary")),
        name="rglru",
    )(x, w_a, cw, cb, wg, bg, lam)


def _ret_kernel(x_ref, cos_ref, sin_ref, w_ref, decay_ref, qw_ref, kw_ref, cd_ref, gn_ref, avg_ref, y_ref,
                state, *, ts):
    C = RET_CHUNK
    n_chunks = ts // C

    @pl.when(pl.program_id(1) == 0)
    def _():
        state[...] = jnp.zeros_like(state)

    xb = x_ref[0].astype(BF16)
    cos = cos_ref[0]
    sin = sin_ref[0]
    lo = (lax.broadcasted_iota(jnp.int32, cos.shape, 1) % RET_DK) < RET_DK // 2
    first = lax.broadcasted_iota(jnp.int32, (C, 128), 1) < RET_DK
    same_head = (lax.broadcasted_iota(jnp.int32, (128, 128), 0) < RET_DK) == (
        lax.broadcasted_iota(jnp.int32, (128, 128), 1) < RET_DK)

    def group_mean(val):
        hi = val.astype(BF16)
        lo_part = (val - hi.astype(F32)).astype(BF16)
        return _dot(jnp.concatenate([hi, lo_part], axis=1), avg_ref[...])

    pairs = range(HEADS // 2)
    chunks = range(n_chunks)
    tiles = [(p, c) for p in pairs for c in chunks]
    lanes = [slice(p * 128, (p + 1) * 128) for p in pairs]
    z = _dot(xb, w_ref[...])
    qb = [_rope_lanes(z[:, p * 512:p * 512 + 128], cos, sin, lo, RET_DK // 2) for p in pairs]
    kb = [_rope_lanes(z[:, p * 512 + 128:p * 512 + 256], cos, sin, lo, RET_DK // 2) for p in pairs]
    vb = [z[:, p * 512 + 256:p * 512 + 384].astype(BF16) for p in pairs]
    rows = [slice(c * C, (c + 1) * C) for c in chunks]
    q2 = {(p, c): jnp.concatenate([jnp.where(first, qb[p][rows[c]], 0.0), jnp.where(first, 0.0, qb[p][rows[c]])],
                                  axis=0).astype(BF16) for p, c in tiles}
    scores = {(p, c): _dot_nt(q2[p, c], kb[p][rows[c]].astype(BF16)) * decay_ref[p] for p, c in tiles}
    inner2 = {(p, c): _dot(scores[p, c].astype(BF16), vb[p][rows[c]]) for p, c in tiles}
    inner = {t: jnp.where(first, inner2[t][:C], inner2[t][C:]) for t in tiles}
    kv = {(p, c): jnp.where(same_head, _dot_tn((kb[p][rows[c]] * kw_ref[:, lanes[p]]).astype(BF16), vb[p][rows[c]]),
                            0.0) for p, c in tiles}
    q_weighted = {(p, c): (qb[p][rows[c]] * qw_ref[:, lanes[p]]).astype(BF16) for p, c in tiles}
    s_prev = [state[p] for p in pairs]
    outs = {}
    for c in chunks:
        for p in pairs:
            outs[p, c] = inner[p, c] + _dot(q_weighted[p, c], s_prev[p].astype(BF16))
            s_prev[p] = cd_ref[:, lanes[p]] * s_prev[p] + kv[p, c]
    for p in pairs:
        state[p] = s_prev[p]
    o = [jnp.concatenate([outs[p, c] for c in chunks], axis=0) for p in pairs]
    mean = [group_mean(o[p]) for p in pairs]
    dev = [o[p] - mean[p] for p in pairs]
    var = [group_mean(dev[p] * dev[p]) for p in pairs]
    for p in pairs:
        half_g = z[:, p * 512 + 384:(p + 1) * 512]
        y = dev[p] * lax.rsqrt(var[p] + LN_EPS) * gn_ref[:, lanes[p]] * (half_g * jnp.tanh(half_g) + half_g)
        y_ref[0, :, lanes[p]] = y.astype(y_ref.dtype)


def _retention(x, cos, sin, layer, w_a, decay, qw, kw, cd, gn, avg, ts=1024):
    B, S, _ = x.shape
    tok = lambda c: pl.BlockSpec((1, ts, c), lambda b, s: (b, s, 0))
    return pl.pallas_call(
        functools.partial(_ret_kernel, ts=ts),
        out_shape=jax.ShapeDtypeStruct((B, S, WIDTH), BF16),
        grid=(B, S // ts),
        in_specs=[tok(D_MODEL), tok(128), tok(128), _layer_spec(w_a, layer, 4 * WIDTH, 0),
                  _const_spec(decay.shape), _const_spec(qw.shape), _const_spec(kw.shape), _const_spec(cd.shape),
                  _layer_spec(gn, layer), _const_spec(avg.shape)],
        out_specs=tok(WIDTH),
        scratch_shapes=[pltpu.VMEM((HEADS // 2, 128, 128), F32)],
        compiler_params=_params(("parallel", "arbitrary")),
        name="retention",
    )(x, cos, sin, w_a, decay, qw, kw, cd, gn, avg)


def _merge_kernel(x_ref, o_ref, yl_ref, yr_ref, wg_ref, bm_ref, wb_ref, wo_ref, lg_ref, lb_ref, out_ref):
    tm = x_ref.shape[1]
    blocks = [slice(r, r + MERGE_ROWS) for r in range(0, tm, MERGE_ROWS)]
    zg = [_dot(x_ref[0, rs, :].astype(BF16), wg_ref[...]) for rs in blocks]
    mixed = []
    for r, rs in enumerate(blocks):
        half_g = zg[r][:, :WIDTH]
        y_mla = (o_ref[0, rs, :].astype(F32) * (half_g * jnp.tanh(half_g) + half_g)).astype(BF16)
        ys = (y_mla, yl_ref[0, rs, :], yr_ref[0, rs, :])
        total = None
        for n in range(N_BRANCHES):
            lo = WIDTH + n * D_MODEL
            gate = 0.5 * jnp.tanh(zg[r][:, lo:lo + D_MODEL] + bm_ref[:, n * D_MODEL:(n + 1) * D_MODEL]) + 0.5
            term = gate * _dot(ys[n], wb_ref[n])
            total = term if total is None else total + term
        mixed.append(total.astype(BF16))
    proj = [_dot(mixed[r], wo_ref[...]) for r in range(len(blocks))]
    for r, rs in enumerate(blocks):
        xf = ALPHA * x_ref[0, rs, :] + proj[r]
        mu = jnp.mean(xf, -1, keepdims=True)
        var = jnp.mean(jnp.square(xf - mu), -1, keepdims=True)
        out_ref[0, rs, :] = (xf - mu) * lax.rsqrt(var + LN_EPS) * lg_ref[...] + lb_ref[...]


def _merge(x, o, yl, yr, layer, wg, bm, wb, wo, lg, lb, tm=1024):
    B, S, _ = x.shape
    tok = lambda c: pl.BlockSpec((1, tm, c), lambda b, s: (b, s, 0))
    return pl.pallas_call(
        _merge_kernel,
        out_shape=jax.ShapeDtypeStruct((B, S, D_MODEL), F32),
        grid=(B, S // tm),
        in_specs=[tok(D_MODEL), tok(WIDTH), tok(WIDTH), tok(WIDTH),
                  _layer_spec(wg, layer), _layer_spec(bm, layer), _layer_spec(wb, layer),
                  _layer_spec(wo, layer), _layer_spec(lg, layer), _layer_spec(lb, layer)],
        out_specs=tok(D_MODEL),
        compiler_params=_params(("parallel", "parallel")),
        name="merge_out_ln",
    )(x, o, yl, yr, wg, bm, wb, wo, lg, lb)


def _rope_patterns():
    lane = np.arange(128)
    f32 = lambda a: np.asarray(a, np.float32)[None, :]
    half = MLA_ROPE // 2
    mla_inv = ROPE_BASE ** (-np.arange(half, dtype=np.float64) / half)
    in_rope = (lane >= MLA_NOPE) & (lane < MLA_NOPE + MLA_ROPE)
    mla_sgn = f32(np.where(in_rope, np.where(lane < MLA_NOPE + half, -1.0, 1.0), 0.0))
    mla_base = f32(np.where(lane < MLA_NOPE, 1.0, 0.0))
    half = RET_DK // 2
    ret_inv = ROPE_BASE ** (-np.arange(half, dtype=np.float64) / half)
    ret_sgn = f32(np.where((lane % RET_DK) < half, -1.0, 1.0))
    freq = f32(np.concatenate([ret_inv, mla_inv, np.zeros(128 - ret_inv.size - mla_inv.size)]))
    return freq, mla_sgn, mla_base, ret_sgn


def _retention_constants():
    C = RET_CHUNK
    log_gamma = np.log1p(-np.exp2(-5.0 - np.arange(HEADS, dtype=np.float64)))
    idx = np.arange(C, dtype=np.float64)
    diff = idx[:, None] - idx[None, :]
    decay = np.where(diff[None] >= 0, np.exp(diff[None] * log_gamma[:, None, None]), 0.0)
    k_w = np.exp((C - 1.0 - idx)[:, None] * log_gamma[None, :])
    q_w = np.exp((idx + 1.0)[:, None] * log_gamma[None, :])
    chunk_decay = np.exp(C * log_gamma)
    expand = lambda t: np.repeat(t, RET_DK, axis=-1).astype(np.float32)
    lane_head = np.arange(128) // RET_DK
    avg = np.where(lane_head[:, None] == lane_head[None, :], 1.0 / RET_DK, 0.0)
    avg = jnp.asarray(np.concatenate([avg, avg], axis=0), BF16)
    return (decay.reshape(HEADS // 2, 2 * C, C).astype(np.float32), expand(q_w), expand(k_w),
            expand(chunk_decay[None, :]), avg)


def _block_diag(w):
    d, n, a, b = w.shape
    return (jnp.eye(n, dtype=w.dtype)[None, :, None, :, None] * w[:, :, :, None, :]).reshape(d, n * a, n * b)


def _split_in_proj_kernel(wt_ref, wa_ref, wgm_ref, wlat_ref):
    def put(dst_ref, dst_col, src_row, scale=None):
        t = wt_ref[src_row:src_row + 128, :].T
        dst_ref[:, dst_col:dst_col + 128] = (t if scale is None else scale * t).astype(BF16)

    lat_end = MLA_Q_LORA + MLA_KV_LORA
    base = lat_end + MLA_ROPE
    g_mla, lru_u, g_lru, r_q, r_k, r_v, g_ret, merge = (base + n * WIDTH for n in range(8))
    for c in range(WIDTH // 128):
        put(wgm_ref, c * 128, g_mla + c * 128, 0.5)
        put(wa_ref, 4 * WIDTH + c * 128, lru_u + c * 128)
        put(wa_ref, 5 * WIDTH + c * 128, g_lru + c * 128, 0.5)
    for c in range(N_BRANCHES * D_MODEL // 128):
        put(wgm_ref, WIDTH + c * 128, merge + c * 128, 0.5)
    for p in range(HEADS // 2):
        for n, (src, scale) in enumerate(((r_q, None), (r_k, RET_DK ** -0.5), (r_v, None), (g_ret, 0.5))):
            put(wa_ref, p * 4 * 128 + n * 128, src + p * 128, scale)
    for c in range(lat_end // 128):
        put(wlat_ref, c * 128, c * 128)
    half = MLA_ROPE // 2
    lanes = wt_ref.shape[1]
    k_pe = jnp.concatenate([jnp.zeros((MLA_NOPE, lanes), F32), wt_ref[lat_end:base, :], wt_ref[lat_end:lat_end + half, :],
                            jnp.zeros((HEAD_PAD - MLA_NOPE - MLA_ROPE - half, lanes), F32)], axis=0)
    wlat_ref[:, lat_end:lat_end + HEAD_PAD] = k_pe.T.astype(BF16)


def _split_in_proj(w_in, lanes=256):
    w_t = jnp.swapaxes(w_in, 1, 2)
    in_cols = w_t.shape[1]
    out = lambda n: jax.ShapeDtypeStruct((DEPTH, D_MODEL, n), BF16)
    spec = lambda n: pl.BlockSpec((None, lanes, n), lambda l, r: (l, r, 0))
    widths = (6 * WIDTH, WIDTH + N_BRANCHES * D_MODEL, LAT_COLS)
    return pl.pallas_call(
        _split_in_proj_kernel,
        out_shape=tuple(out(n) for n in widths),
        grid=(DEPTH, D_MODEL // lanes),
        in_specs=[pl.BlockSpec((None, in_cols, lanes), lambda l, r: (l, 0, r))],
        out_specs=tuple(spec(n) for n in widths),
        compiler_params=_params(("parallel", "parallel")),
        name="split_in_proj",
    )(w_t)


def _prepare_weights(w_in, mla_w_uq, mla_w_ukv, lru_w_r, lru_w_i):
    assert w_in.shape[-1] == sum(IN_SIZES)
    w_a, w_gm, w_lat = _split_in_proj(w_in)
    half = MLA_ROPE // 2
    pad = HEAD_PAD - MLA_NOPE - MLA_ROPE - half

    w_uq = mla_w_uq.astype(BF16).reshape(DEPTH, MLA_Q_LORA, HEADS, MLA_NOPE + MLA_ROPE)
    w_uq = jnp.concatenate([w_uq, w_uq[..., MLA_NOPE:MLA_NOPE + half],
                            jnp.zeros((DEPTH, MLA_Q_LORA, HEADS, pad), BF16)], axis=3)
    w_uq = w_uq.reshape(DEPTH, MLA_Q_LORA, HEADS * HEAD_PAD)
    w_ukv = mla_w_ukv.astype(BF16).reshape(DEPTH, MLA_KV_LORA, HEADS, MLA_NOPE + MLA_V)
    tile_pad = ((0, 0), (0, 0), (0, 0), (0, HEAD_PAD - MLA_NOPE))
    w_ukv = jnp.concatenate([jnp.pad(w_ukv[..., :MLA_NOPE], tile_pad).reshape(DEPTH, MLA_KV_LORA, HEADS * HEAD_PAD),
                             jnp.pad(w_ukv[..., MLA_NOPE:], tile_pad).reshape(DEPTH, MLA_KV_LORA, HEADS * HEAD_PAD)],
                            axis=2)
    w_gates = 0.5 * jnp.concatenate([_block_diag(lru_w_r.astype(BF16)), _block_diag(lru_w_i.astype(BF16))], axis=2)
    return w_a, w_gm, w_lat, w_uq, w_ukv, w_gates


def kernel(x, positions, w_in, b_merge, mla_q_norm, mla_w_uq, mla_kv_norm, mla_w_ukv, lru_conv_w, lru_conv_b,
           lru_w_r, lru_b_r, lru_w_i, lru_b_i, lru_lambda, ret_gn_g, w_branch, w_out, ln_g, ln_b):
    pos = jnp.broadcast_to(positions.astype(F32)[:, :, None], positions.shape + (128,))
    mla_cos, mla_sin, ret_cos, ret_sin = _rope_tables(pos, *_rope_patterns())
    decay, q_w, k_w, chunk_decay, avg = _retention_constants()

    w_a, w_gm, w_lat, w_uq, w_ukv, w_gates = _prepare_weights(w_in, mla_w_uq, mla_w_ukv, lru_w_r, lru_w_i)
    w_b = w_branch.astype(BF16)
    w_o = w_out.astype(BF16)
    row = lambda v: v[:, None, :]
    b_gates = row(0.5 * jnp.concatenate([lru_b_r, lru_b_i], axis=1))
    b_m = row(0.5 * b_merge)

    for l in range(DEPTH):
        q, k, v = _mla_proj(x, mla_cos, mla_sin, l, w_lat, row(mla_q_norm), row(mla_kv_norm), w_uq, w_ukv)
        o_mla = _attention(q, k, v)
        y_lru = _lru(x, l, w_a, lru_conv_w, row(lru_conv_b), w_gates, b_gates, row(lru_lambda))
        y_ret = _retention(x, ret_cos, ret_sin, l, w_a, decay, q_w, k_w, chunk_decay, row(ret_gn_g), avg)
        x = _merge(x, o_mla, y_lru, y_ret, l, w_gm, b_m, w_b, w_o, row(ln_g), row(ln_b))
    return x
```

```python
import functools

import jax
import jax.numpy as jnp
import numpy as np
from jax import lax
from jax.experimental import pallas as pl
from jax.experimental.pallas import tpu as pltpu

F32 = jnp.float32
BF16 = jnp.bfloat16

D_MODEL = 1024
DEPTH = 2
HEADS = 8
MLA_Q_LORA = 384
MLA_KV_LORA = 256
MLA_NOPE = 64
MLA_ROPE = 32
MLA_V = 64
HEAD_PAD = 128
WIDTH = 512
LRU_CONV = 4
LRU_C = 8.0
LRU_ROWS = 256
MERGE_ROWS = 256
PROJ_ROWS = 256
RET_DK = 64
RET_CHUNK = 128
N_BRANCHES = 3
ROPE_BASE = 10000.0
LN_EPS = 1e-5
RMS_EPS = 1e-6
ALPHA = (2 * DEPTH) ** 0.25
ATTN_SCALE = (MLA_NOPE + MLA_ROPE) ** -0.5
NEG = -1e30
LOG2E = 1.4426950408889634

IN_SIZES = (MLA_Q_LORA, MLA_KV_LORA, MLA_ROPE, WIDTH, WIDTH, WIDTH, WIDTH, WIDTH, WIDTH, WIDTH,
            N_BRANCHES * D_MODEL)
LAT_COLS = MLA_Q_LORA + MLA_KV_LORA + HEAD_PAD

VMEM_LIMIT = 56 * 1024 * 1024


def _const_spec(shape):
    return pl.BlockSpec(shape, lambda *_: (0,) * len(shape))


def _layer_spec(arr, layer, width=None, block=0):
    shape = arr.shape[1:]
    if width is not None:
        shape = shape[:-1] + (width,)
    return pl.BlockSpec((None,) + shape, lambda *_: (layer,) + (0,) * (len(shape) - 1) + (block,))


def _params(sem, vmem=VMEM_LIMIT):
    return pltpu.CompilerParams(dimension_semantics=sem, vmem_limit_bytes=vmem)


def _dot(a, b):
    return jnp.dot(a, b, preferred_element_type=F32)


def _dot_nt(a, b):
    return lax.dot_general(a, b, (((1,), (1,)), ((), ())), preferred_element_type=F32)


def _dot_tn(a, b):
    return lax.dot_general(a, b, (((0,), (0,)), ((), ())), preferred_element_type=F32)


def _rope_table_kernel(pos_ref, freq_ref, msgn_ref, mbase_ref, rsgn_ref, mcos_ref, msin_ref, rcos_ref, rsin_ref):
    rows = pos_ref.shape[1] // 2
    lane = lax.broadcasted_iota(jnp.int32, (rows, 128), 1)
    ang = jnp.where(lane < 64, pos_ref[0, :rows, :], pos_ref[0, rows:, :]) * freq_ref[...]
    first_half = (lane >= MLA_NOPE) & (lane < MLA_NOPE + MLA_ROPE // 2)
    second_half = (lane >= MLA_NOPE + MLA_ROPE // 2) & (lane < MLA_NOPE + MLA_ROPE)
    for trig, sgn_m, sgn_r, base, m_ref, r_ref in ((jnp.cos(ang), None, None, mbase_ref[...], mcos_ref, rcos_ref),
                                                   (jnp.sin(ang), msgn_ref[...], rsgn_ref[...], 0.0, msin_ref, rsin_ref)):
        for part in range(2):
            src = trig if part == 0 else pltpu.roll(trig, 64, 1)
            mla = jnp.where(first_half, pltpu.roll(src, 32, 1), jnp.where(second_half, pltpu.roll(src, 48, 1), base))
            ret = jnp.where(lane < 32, src, pltpu.roll(src, 32, 1))
            ret = jnp.where(lane < 64, ret, pltpu.roll(ret, 64, 1))
            out_rows = slice(part * rows, (part + 1) * rows)
            m_ref[0, out_rows, :] = mla if sgn_m is None else mla * sgn_m
            r_ref[0, out_rows, :] = ret if sgn_r is None else ret * sgn_r


def _rope_tables(positions, freq, mla_sgn, mla_base, ret_sgn, ts=1024):
    B, S, _ = positions.shape
    tab = jax.ShapeDtypeStruct((B, S, 128), F32)
    tok = pl.BlockSpec((1, ts, 128), lambda b, s: (b, s, 0))
    return pl.pallas_call(
        _rope_table_kernel,
        out_shape=(tab, tab, tab, tab),
        grid=(B, S // ts),
        in_specs=[tok] + [_const_spec((1, 128))] * 4,
        out_specs=(tok, tok, tok, tok),
        compiler_params=_params(("parallel", "parallel")),
        name="rope_tables",
    )(positions, freq, mla_sgn, mla_base, ret_sgn)


def _rope_lanes(x, cos, sin, lo_mask, half):
    swapped = jnp.where(lo_mask, pltpu.roll(x, 128 - half, 1), pltpu.roll(x, half, 1))
    return x * cos + swapped * sin


def _mla_proj_kernel(x_ref, cos_ref, sin_ref, wlat_ref, qn_ref, kvn_ref, wuq_ref, wukv_ref,
                     q_ref, k_ref, v_ref):
    tm = x_ref.shape[1]
    blocks = [slice(r, r + PROJ_ROWS) for r in range(0, tm, PROJ_ROWS)]

    def rms(v, g):
        return v * lax.rsqrt(jnp.mean(jnp.square(v), -1, keepdims=True) + RMS_EPS) * g

    def rotate(t, c, s):
        return t * c + pltpu.roll(t, 128 - MLA_ROPE // 2, 1) * s

    lat = [_dot(x_ref[0, rs, :].astype(BF16), wlat_ref[...]) for rs in blocks]
    q = [_dot(rms(t[:, :MLA_Q_LORA], qn_ref[...]).astype(BF16), wuq_ref[...]) for t in lat]
    kv = [_dot(rms(t[:, MLA_Q_LORA:MLA_Q_LORA + MLA_KV_LORA], kvn_ref[...]).astype(BF16), wukv_ref[...])
          for t in lat]
    lane = lax.broadcasted_iota(jnp.int32, (PROJ_ROWS, HEAD_PAD), 1)
    ones_lane = jnp.where(lane == MLA_V, 1.0, 0.0)
    for r, rs in enumerate(blocks):
        cos = cos_ref[0, rs, :]
        sin = sin_ref[0, rs, :]
        cos_q = cos * (ATTN_SCALE * LOG2E)
        sin_q = sin * (ATTN_SCALE * LOG2E)
        k_rot = rotate(lat[r][:, MLA_Q_LORA + MLA_KV_LORA:], cos, sin)
        for h in range(HEADS):
            sl = slice(h * HEAD_PAD, (h + 1) * HEAD_PAD)
            vl = slice((HEADS + h) * HEAD_PAD, (HEADS + h + 1) * HEAD_PAD)
            q_ref[0, rs, sl] = rotate(q[r][:, sl], cos_q, sin_q).astype(BF16)
            k_ref[0, rs, sl] = (kv[r][:, sl] + k_rot).astype(BF16)
            v_ref[0, rs, sl] = (kv[r][:, vl] + ones_lane).astype(BF16)


def _mla_proj(x, cos, sin, layer, wlat, qn, kvn, wuq, wukv, tm=1024):
    B, S, _ = x.shape
    tok = lambda c: pl.BlockSpec((1, tm, c), lambda b, s: (b, s, 0))
    out = jax.ShapeDtypeStruct((B, S, HEADS * HEAD_PAD), BF16)
    return pl.pallas_call(
        _mla_proj_kernel,
        out_shape=(out, out, out),
        grid=(B, S // tm),
        in_specs=[tok(D_MODEL), tok(128), tok(128),
                  _layer_spec(wlat, layer),
                  _layer_spec(qn, layer), _layer_spec(kvn, layer), _layer_spec(wuq, layer), _layer_spec(wukv, layer)],
        out_specs=(tok(HEADS * HEAD_PAD), tok(HEADS * HEAD_PAD), tok(HEADS * HEAD_PAD)),
        compiler_params=_params(("parallel", "parallel")),
        name="mla_proj",
    )(x, cos, sin, wlat, qn, kvn, wuq, wukv)


def _attn_kernel(q_ref, k_ref, v_ref, o_ref, m_s, acc_s, s_a, s_b, *, tq, tk):
    i = pl.program_id(2)
    ratio = tq // tk
    assert ratio == 2
    m_s[...] = jnp.full(m_s.shape, NEG, F32)
    acc_s[...] = jnp.zeros(acc_s.shape, F32)
    heads = [slice(hh * HEAD_PAD, (hh + 1) * HEAD_PAD) for hh in range(2)]

    top, bottom = slice(0, tk), slice(tk, tq)

    def scores(j, hh, rows):
        start = pl.multiple_of(j * tk, tk)
        return _dot_nt(q_ref[0, rows, heads[hh]], k_ref[0, pl.ds(start, tk), heads[hh]])

    def accumulate(j, hh, s, rows):
        start = pl.multiple_of(j * tk, tk)
        m = m_s[hh, rows, :]
        m_new = jnp.maximum(m, jnp.max(s, -1, keepdims=True))
        p = jnp.concatenate([jnp.exp2(s[:, c * 128:(c + 1) * 128] - m_new) for c in range(tk // 128)], axis=1)
        pv = _dot(p.astype(BF16), v_ref[0, pl.ds(start, tk), heads[hh]])
        acc_s[hh, rows, :] = jnp.exp2(m - m_new) * acc_s[hh, rows, :] + pv
        m_s[hh, rows, :] = m_new

    for rows in (top, bottom):
        for hh in range(2):
            s_a[hh, rows, :] = scores(0, hh, rows)

    def body(n, _):
        for cur, nxt, j in ((s_a, s_b, 2 * n), (s_b, s_a, 2 * n + 1)):
            for rows in (top, bottom):
                for hh in range(2):
                    nxt[hh, rows, :] = scores(j + 1, hh, rows)
                    accumulate(j, hh, cur[hh, rows, :], rows)
        return 0

    lax.fori_loop(0, i, body, 0)
    causal = (lax.broadcasted_iota(jnp.int32, (tk, tk), 1) <= lax.broadcasted_iota(jnp.int32, (tk, tk), 0))
    for hh in range(2):
        s_b[hh, bottom, :] = scores(2 * i + 1, hh, bottom)
        accumulate(2 * i, hh, jnp.where(causal, s_a[hh, top, :], NEG), top)
    for hh in range(2):
        accumulate(2 * i, hh, s_a[hh, bottom, :], bottom)
    for hh in range(2):
        accumulate(2 * i + 1, hh, jnp.where(causal, s_b[hh, bottom, :], NEG), bottom)
    for hh in range(2):
        acc = acc_s[hh]
        o = acc[:, :MLA_V] / acc[:, MLA_V:MLA_V + 1]
        o_ref[0, :, hh * MLA_V:(hh + 1) * MLA_V] = o.astype(o_ref.dtype)


def _attention(q, k, v, tq=1024, tk=512):
    B, S, _ = q.shape
    return pl.pallas_call(
        functools.partial(_attn_kernel, tq=tq, tk=tk),
        out_shape=jax.ShapeDtypeStruct((B, S, WIDTH), BF16),
        grid=(B, HEADS // 2, S // tq),
        in_specs=[pl.BlockSpec((1, tq, 2 * HEAD_PAD), lambda b, h, i: (b, i, h)),
                  pl.BlockSpec((1, S, 2 * HEAD_PAD), lambda b, h, i: (b, 0, h)),
                  pl.BlockSpec((1, S, 2 * HEAD_PAD), lambda b, h, i: (b, 0, h))],
        out_specs=pl.BlockSpec((1, tq, 2 * MLA_V), lambda b, h, i: (b, i, h)),
        scratch_shapes=[pltpu.VMEM((2, tq, HEAD_PAD), F32), pltpu.VMEM((2, tq, HEAD_PAD), F32),
                        pltpu.VMEM((2, tq, tk), F32), pltpu.VMEM((2, tq, tk), F32)],
        compiler_params=_params(("parallel", "parallel", "arbitrary")),
        name="mla_attention",
    )(q, k, v)


def _lru_kernel(x_ref, w_ref, cw_ref, cb_ref, wg_ref, bg_ref, lam_ref, y_ref, ubuf, hcar, h_s, *, ts):
    @pl.when(pl.program_id(1) == 0)
    def _():
        ubuf[0:8, :] = jnp.zeros((8, WIDTH), F32)
        hcar[...] = jnp.zeros_like(hcar)

    nb = ts // LRU_ROWS
    blocks = [slice(r * LRU_ROWS, (r + 1) * LRU_ROWS) for r in range(nb)]
    z = [_dot(x_ref[0, rs, :].astype(BF16), w_ref[...]) for rs in blocks]
    xc = []
    for r, rs in enumerate(blocks):
        u = z[r][:, :WIDTH]
        lo = 8 + r * LRU_ROWS
        ubuf[lo:lo + LRU_ROWS, :] = u
        acc = cb_ref[...] + cw_ref[3:4, :] * u
        for j in range(1, LRU_CONV):
            acc = acc + cw_ref[3 - j:4 - j, :] * ubuf[lo - j:lo - j + LRU_ROWS, :]
        xc.append(acc)
    ubuf[0:8, :] = z[nb - 1][LRU_ROWS - 8:, :WIDTH]

    th = [jnp.tanh(_dot(xc[r].astype(BF16), wg_ref[...]) + bg_ref[...]) for r in range(nb)]
    lam = lam_ref[...]
    softplus_neg = jnp.maximum(-lam, 0.0) + jnp.log1p(jnp.exp(-jnp.abs(lam)))
    c1 = (-0.5 * LRU_C) * softplus_neg
    groups = LRU_ROWS // 8
    sub = lax.broadcasted_iota(jnp.int32, (groups, 8, WIDTH), 1)
    a3, b3 = [], []
    for r in range(nb):
        log_a = th[r][:, :WIDTH] * c1 + c1
        a = jnp.exp(log_a)
        t = jnp.tanh(log_a)
        b_sq = -t * (1.0 + a * a)
        root = jnp.where(b_sq > 0.0, b_sq * lax.rsqrt(b_sq), 0.0)
        b = root * ((th[r][:, WIDTH:] + 1.0) * (0.5 * xc[r]))
        a = a.reshape(groups, 8, WIDTH)
        b = b.reshape(groups, 8, WIDTH)
        for d in (1, 2, 4):
            keep = sub >= d
            b = a * jnp.where(keep, pltpu.roll(b, d, 1), 0.0) + b
            a = a * jnp.where(keep, pltpu.roll(a, d, 1), 1.0)
        a3.append(a)
        b3.append(b)
    carry = hcar[...]
    for r in range(nb):
        for n in range(groups):
            h = a3[r][n] * carry + b3[r][n]
            h_s[r * LRU_ROWS + n * 8:r * LRU_ROWS + (n + 1) * 8, :] = h
            carry = jnp.broadcast_to(h[7:8, :], (8, WIDTH))
    hcar[...] = carry
    for r, rs in enumerate(blocks):
        half_g = z[r][:, WIDTH:]
        y_ref[0, rs, :] = (h_s[rs, :] * (half_g * jnp.tanh(half_g) + half_g)).astype(y_ref.dtype)


def _lru(x, layer, w_a, cw, cb, wg, bg, lam, ts=1024):
    B, S, _ = x.shape
    return pl.pallas_call(
        functools.partial(_lru_kernel, ts=ts),
        out_shape=jax.ShapeDtypeStruct((B, S, WIDTH), BF16),
        grid=(B, S // ts),
        in_specs=[pl.BlockSpec((1, ts, D_MODEL), lambda b, s: (b, s, 0)),
                  _layer_spec(w_a, layer, 2 * WIDTH, 2), _layer_spec(cw, layer), _layer_spec(cb, layer),
                  _layer_spec(wg, layer), _layer_spec(bg, layer), _layer_spec(lam, layer)],
        out_specs=pl.BlockSpec((1, ts, WIDTH), lambda b, s: (b, s, 0)),
        scratch_shapes=[pltpu.VMEM((ts + 8, WIDTH), F32), pltpu.VMEM((8, WIDTH), F32),
                        pltpu.VMEM((ts, WIDTH), F32)],
        compiler_params=_params(("parallel", "arbitrary")),
        name="rglru",
    )(x, w_a, cw, cb, wg, bg, lam)


def _ret_kernel(x_ref, cos_ref, sin_ref, w_ref, decay_ref, qw_ref, kw_ref, cd_ref, gn_ref, avg_ref, y_ref,
                state, *, ts):
    C = RET_CHUNK
    n_chunks = ts // C

    @pl.when(pl.program_id(1) == 0)
    def _():
        state[...] = jnp.zeros_like(state)

    xb = x_ref[0].astype(BF16)
    cos = cos_ref[0]
    sin = sin_ref[0]
    lo = (lax.broadcasted_iota(jnp.int32, cos.shape, 1) % RET_DK) < RET_DK // 2
    first = lax.broadcasted_iota(jnp.int32, (C, 128), 1) < RET_DK
    same_head = (lax.broadcasted_iota(jnp.int32, (128, 128), 0) < RET_DK) == (
        lax.broadcasted_iota(jnp.int32, (128, 128), 1) < RET_DK)

    def group_mean(val):
        hi = val.astype(BF16)
        lo_part = (val - hi.astype(F32)).astype(BF16)
        return _dot(jnp.concatenate([hi, lo_part], axis=1), avg_ref[...])

    pairs = range(HEADS // 2)
    chunks = range(n_chunks)
    tiles = [(p, c) for p in pairs for c in chunks]
    lanes = [slice(p * 128, (p + 1) * 128) for p in pairs]
    z = _dot(xb, w_ref[...])
    qb = [_rope_lanes(z[:, p * 512:p * 512 + 128], cos, sin, lo, RET_DK // 2) for p in pairs]
    kb = [_rope_lanes(z[:, p * 512 + 128:p * 512 + 256], cos, sin, lo, RET_DK // 2) for p in pairs]
    vb = [z[:, p * 512 + 256:p * 512 + 384].astype(BF16) for p in pairs]
    rows = [slice(c * C, (c + 1) * C) for c in chunks]
    q2 = {(p, c): jnp.concatenate([jnp.where(first, qb[p][rows[c]], 0.0), jnp.where(first, 0.0, qb[p][rows[c]])],
                                  axis=0).astype(BF16) for p, c in tiles}
    scores = {(p, c): _dot_nt(q2[p, c], kb[p][rows[c]].astype(BF16)) * decay_ref[p] for p, c in tiles}
    inner2 = {(p, c): _dot(scores[p, c].astype(BF16), vb[p][rows[c]]) for p, c in tiles}
    inner = {t: jnp.where(first, inner2[t][:C], inner2[t][C:]) for t in tiles}
    kv = {(p, c): jnp.where(same_head, _dot_tn((kb[p][rows[c]] * kw_ref[:, lanes[p]]).astype(BF16), vb[p][rows[c]]),
                            0.0) for p, c in tiles}
    q_weighted = {(p, c): (qb[p][rows[c]] * qw_ref[:, lanes[p]]).astype(BF16) for p, c in tiles}
    s_prev = [state[p] for p in pairs]
    outs = {}
    for c in chunks:
        for p in pairs:
            outs[p, c] = inner[p, c] + _dot(q_weighted[p, c], s_prev[p].astype(BF16))
            s_prev[p] = cd_ref[:, lanes[p]] * s_prev[p] + kv[p, c]
    for p in pairs:
        state[p] = s_prev[p]
    o = [jnp.concatenate([outs[p, c] for c in chunks], axis=0) for p in pairs]
    mean = [group_mean(o[p]) for p in pairs]
    dev = [o[p] - mean[p] for p in pairs]
    var = [group_mean(dev[p] * dev[p]) for p in pairs]
    for p in pairs:
        half_g = z[:, p * 512 + 384:(p + 1) * 512]
        y = dev[p] * lax.rsqrt(var[p] + LN_EPS) * gn_ref[:, lanes[p]] * (half_g * jnp.tanh(half_g) + half_g)
        y_ref[0, :, lanes[p]] = y.astype(y_ref.dtype)


def _retention(x, cos, sin, layer, w_a, decay, qw, kw, cd, gn, avg, ts=1024):
    B, S, _ = x.shape
    tok = lambda c: pl.BlockSpec((1, ts, c), lambda b, s: (b, s, 0))
    return pl.pallas_call(
        functools.partial(_ret_kernel, ts=ts),
        out_shape=jax.ShapeDtypeStruct((B, S, WIDTH), BF16),
        grid=(B, S // ts),
        in_specs=[tok(D_MODEL), tok(128), tok(128), _layer_spec(w_a, layer, 4 * WIDTH, 0),
                  _const_spec(decay.shape), _const_spec(qw.shape), _const_spec(kw.shape), _const_spec(cd.shape),
                  _layer_spec(gn, layer), _const_spec(avg.shape)],
        out_specs=tok(WIDTH),
        scratch_shapes=[pltpu.VMEM((HEADS // 2, 128, 128), F32)],
        compiler_params=_params(("parallel", "arbitrary")),
        name="retention",
    )(x, cos, sin, w_a, decay, qw, kw, cd, gn, avg)


def _merge_kernel(x_ref, o_ref, yl_ref, yr_ref, wg_ref, bm_ref, wb_ref, wo_ref, lg_ref, lb_ref, out_ref):
    tm = x_ref.shape[1]
    blocks = [slice(r, r + MERGE_ROWS) for r in range(0, tm, MERGE_ROWS)]
    zg = [_dot(x_ref[0, rs, :].astype(BF16), wg_ref[...]) for rs in blocks]
    mixed = []
    for r, rs in enumerate(blocks):
        half_g = zg[r][:, :WIDTH]
        y_mla = (o_ref[0, rs, :].astype(F32) * (half_g * jnp.tanh(half_g) + half_g)).astype(BF16)
        ys = (y_mla, yl_ref[0, rs, :], yr_ref[0, rs, :])
        total = None
        for n in range(N_BRANCHES):
            lo = WIDTH + n * D_MODEL
            gate = 0.5 * jnp.tanh(zg[r][:, lo:lo + D_MODEL] + bm_ref[:, n * D_MODEL:(n + 1) * D_MODEL]) + 0.5
            term = gate * _dot(ys[n], wb_ref[n])
            total = term if total is None else total + term
        mixed.append(total.astype(BF16))
    proj = [_dot(mixed[r], wo_ref[...]) for r in range(len(blocks))]
    for r, rs in enumerate(blocks):
        xf = ALPHA * x_ref[0, rs, :] + proj[r]
        mu = jnp.mean(xf, -1, keepdims=True)
        var = jnp.mean(jnp.square(xf - mu), -1, keepdims=True)
        out_ref[0, rs, :] = (xf - mu) * lax.rsqrt(var + LN_EPS) * lg_ref[...] + lb_ref[...]


def _merge(x, o, yl, yr, layer, wg, bm, wb, wo, lg, lb, tm=1024):
    B, S, _ = x.shape
    tok = lambda c: pl.BlockSpec((1, tm, c), lambda b, s: (b, s, 0))
    return pl.pallas_call(
        _merge_kernel,
        out_shape=jax.ShapeDtypeStruct((B, S, D_MODEL), F32),
        grid=(B, S // tm),
        in_specs=[tok(D_MODEL), tok(WIDTH), tok(WIDTH), tok(WIDTH),
                  _layer_spec(wg, layer), _layer_spec(bm, layer), _layer_spec(wb, layer),
                  _layer_spec(wo, layer), _layer_spec(lg, layer), _layer_spec(lb, layer)],
        out_specs=tok(D_MODEL),
        compiler_params=_params(("parallel", "parallel")),
        name="merge_out_ln",
    )(x, o, yl, yr, wg, bm, wb, wo, lg, lb)


def _rope_patterns():
    lane = np.arange(128)
    f32 = lambda a: np.asarray(a, np.float32)[None, :]
    half = MLA_ROPE // 2
    mla_inv = ROPE_BASE ** (-np.arange(half, dtype=np.float64) / half)
    in_rope = (lane >= MLA_NOPE) & (lane < MLA_NOPE + MLA_ROPE)
    mla_sgn = f32(np.where(in_rope, np.where(lane < MLA_NOPE + half, -1.0, 1.0), 0.0))
    mla_base = f32(np.where(lane < MLA_NOPE, 1.0, 0.0))
    half = RET_DK // 2
    ret_inv = ROPE_BASE ** (-np.arange(half, dtype=np.float64) / half)
    ret_sgn = f32(np.where((lane % RET_DK) < half, -1.0, 1.0))
    freq = f32(np.tile(np.concatenate([ret_inv, mla_inv, np.zeros(64 - ret_inv.size - mla_inv.size)]), 2))
    return freq, mla_sgn, mla_base, ret_sgn


def _retention_constants():
    C = RET_CHUNK
    log_gamma = np.log1p(-np.exp2(-5.0 - np.arange(HEADS, dtype=np.float64)))
    idx = np.arange(C, dtype=np.float64)
    diff = idx[:, None] - idx[None, :]
    decay = np.where(diff[None] >= 0, np.exp(diff[None] * log_gamma[:, None, None]), 0.0)
    k_w = np.exp((C - 1.0 - idx)[:, None] * log_gamma[None, :])
    q_w = np.exp((idx + 1.0)[:, None] * log_gamma[None, :])
    chunk_decay = np.exp(C * log_gamma)
    expand = lambda t: np.repeat(t, RET_DK, axis=-1).astype(np.float32)
    lane_head = np.arange(128) // RET_DK
    avg = np.where(lane_head[:, None] == lane_head[None, :], 1.0 / RET_DK, 0.0)
    avg = jnp.asarray(np.concatenate([avg, avg], axis=0), BF16)
    return (decay.reshape(HEADS // 2, 2 * C, C).astype(np.float32), expand(q_w), expand(k_w),
            expand(chunk_decay[None, :]), avg)


def _block_diag(w):
    d, n, a, b = w.shape
    return (jnp.eye(n, dtype=w.dtype)[None, :, None, :, None] * w[:, :, :, None, :]).reshape(d, n * a, n * b)


def _split_in_proj_kernel(wt_ref, wa_ref, wgm_ref, wlat_ref):
    def put(dst_ref, dst_col, src_row, scale=None):
        t = wt_ref[src_row:src_row + 128, :].T
        dst_ref[:, dst_col:dst_col + 128] = (t if scale is None else scale * t).astype(BF16)

    lat_end = MLA_Q_LORA + MLA_KV_LORA
    base = lat_end + MLA_ROPE
    g_mla, lru_u, g_lru, r_q, r_k, r_v, g_ret, merge = (base + n * WIDTH for n in range(8))
    for c in range(WIDTH // 128):
        put(wgm_ref, c * 128, g_mla + c * 128, 0.5)
        put(wa_ref, 4 * WIDTH + c * 128, lru_u + c * 128)
        put(wa_ref, 5 * WIDTH + c * 128, g_lru + c * 128, 0.5)
    for c in range(N_BRANCHES * D_MODEL // 128):
        put(wgm_ref, WIDTH + c * 128, merge + c * 128, 0.5)
    for p in range(HEADS // 2):
        for n, (src, scale) in enumerate(((r_q, None), (r_k, RET_DK ** -0.5), (r_v, None), (g_ret, 0.5))):
            put(wa_ref, p * 4 * 128 + n * 128, src + p * 128, scale)
    for c in range(lat_end // 128):
        put(wlat_ref, c * 128, c * 128)
    half = MLA_ROPE // 2
    lanes = wt_ref.shape[1]
    k_pe = jnp.concatenate([jnp.zeros((MLA_NOPE, lanes), F32), wt_ref[lat_end:base, :], wt_ref[lat_end:lat_end + half, :],
                            jnp.zeros((HEAD_PAD - MLA_NOPE - MLA_ROPE - half, lanes), F32)], axis=0)
    wlat_ref[:, lat_end:lat_end + HEAD_PAD] = k_pe.T.astype(BF16)


def _split_in_proj(w_in, lanes=256):
    w_t = jnp.swapaxes(w_in, 1, 2)
    in_cols = w_t.shape[1]
    out = lambda n: jax.ShapeDtypeStruct((DEPTH, D_MODEL, n), BF16)
    spec = lambda n: pl.BlockSpec((None, lanes, n), lambda l, r: (l, r, 0))
    widths = (6 * WIDTH, WIDTH + N_BRANCHES * D_MODEL, LAT_COLS)
    return pl.pallas_call(
        _split_in_proj_kernel,
        out_shape=tuple(out(n) for n in widths),
        grid=(DEPTH, D_MODEL // lanes),
        in_specs=[pl.BlockSpec((None, in_cols, lanes), lambda l, r: (l, 0, r))],
        out_specs=tuple(spec(n) for n in widths),
        compiler_params=_params(("parallel", "parallel")),
        name="split_in_proj",
    )(w_t)


def _prepare_weights(w_in, mla_w_uq, mla_w_ukv, lru_w_r, lru_w_i):
    assert w_in.shape[-1] == sum(IN_SIZES)
    w_a, w_gm, w_lat = _split_in_proj(w_in)
    half = MLA_ROPE // 2
    pad = HEAD_PAD - MLA_NOPE - MLA_ROPE - half

    w_uq = mla_w_uq.astype(BF16).reshape(DEPTH, MLA_Q_LORA, HEADS, MLA_NOPE + MLA_ROPE)
    w_uq = jnp.concatenate([w_uq, w_uq[..., MLA_NOPE:MLA_NOPE + half],
                            jnp.zeros((DEPTH, MLA_Q_LORA, HEADS, pad), BF16)], axis=3)
    w_uq = w_uq.reshape(DEPTH, MLA_Q_LORA, HEADS * HEAD_PAD)
    w_ukv = mla_w_ukv.astype(BF16).reshape(DEPTH, MLA_KV_LORA, HEADS, MLA_NOPE + MLA_V)
    tile_pad = ((0, 0), (0, 0), (0, 0), (0, HEAD_PAD - MLA_NOPE))
    w_ukv = jnp.concatenate([jnp.pad(w_ukv[..., :MLA_NOPE], tile_pad).reshape(DEPTH, MLA_KV_LORA, HEADS * HEAD_PAD),
                             jnp.pad(w_ukv[..., MLA_NOPE:], tile_pad).reshape(DEPTH, MLA_KV_LORA, HEADS * HEAD_PAD)],
                            axis=2)
    w_gates = 0.5 * jnp.concatenate([_block_diag(lru_w_r.astype(BF16)), _block_diag(lru_w_i.astype(BF16))], axis=2)
    return w_a, w_gm, w_lat, w_uq, w_ukv, w_gates


def kernel(x, positions, w_in, b_merge, mla_q_norm, mla_w_uq, mla_kv_norm, mla_w_ukv, lru_conv_w, lru_conv_b,
           lru_w_r, lru_b_r, lru_w_i, lru_b_i, lru_lambda, ret_gn_g, w_branch, w_out, ln_g, ln_b):
    pos = jnp.broadcast_to(positions.astype(F32)[:, :, None], positions.shape + (128,))
    mla_cos, mla_sin, ret_cos, ret_sin = _rope_tables(pos, *_rope_patterns())
    decay, q_w, k_w, chunk_decay, avg = _retention_constants()

    w_a, w_gm, w_lat, w_uq, w_ukv, w_gates = _prepare_weights(w_in, mla_w_uq, mla_w_ukv, lru_w_r, lru_w_i)
    w_b = w_branch.astype(BF16)
    w_o = w_out.astype(BF16)
    row = lambda v: v[:, None, :]
    b_gates = row(0.5 * jnp.concatenate([lru_b_r, lru_b_i], axis=1))
    b_m = row(0.5 * b_merge)

    for l in range(DEPTH):
        q, k, v = _mla_proj(x, mla_cos, mla_sin, l, w_lat, row(mla_q_norm), row(mla_kv_norm), w_uq, w_ukv)
        o_mla = _attention(q, k, v)
        y_lru = _lru(x, l, w_a, lru_conv_w, row(lru_conv_b), w_gates, b_gates, row(lru_lambda))
        y_ret = _retention(x, ret_cos, ret_sin, l, w_a, decay, q_w, k_w, chunk_decay, row(ret_gn_g), avg)
        x = _merge(x, o_mla, y_lru, y_ret, l, w_gm, b_m, w_b, w_o, row(ln_g), row(ln_b))
    return x
```
